```python
import math
import jax
import jax.numpy as jnp
from jax import lax
import numpy as np

D_MODEL = 1024
BATCH = 2
SEQ = 8192
DEPTH = 4

GRID_W = 64
CTX_LEN = 256
N_EVEN = (DEPTH + 1) // 2
N_ODD = DEPTH // 2
D_FF = -(-(8 * D_MODEL) // (3 * 256)) * 256
NORM_EPS = 1e-6

D_HY = D_MODEL // 2
HY_ORDER = 2
HY_SHORT = 3
HY_EMB = 33
HY_BANDS = (HY_EMB - 1) // 2
HY_FILTER_ORDER = 64
HY_FAST_DECAY = 0.3
HY_SLOW_DECAY = 1.5
HY_TARGET = 1e-2

MLA_HEADS = 8
MLA_NOPE = 64
MLA_ROPE = 32
MLA_V = 64
MLA_Q_RANK = 384
MLA_KV_RANK = 256
MLA_SCALE = (MLA_NOPE + MLA_ROPE) ** -0.5
ROPE_AXIS = MLA_ROPE // 2
ROPE_BASE = 10000.0
Q_BLOCK = 128

EVEN_IN = (HY_ORDER + 1) * D_HY + MLA_Q_RANK + MLA_KV_RANK + MLA_ROPE
EVEN_MIX = D_HY + MLA_HEADS * MLA_V

D_RNN = 1280
RG_BLOCKS = 10
RG_BW = D_RNN // RG_BLOCKS
RG_CONV = 4
RG_C = 8.0

kernel_name = 'hybrid_hyena_mla_rglru_flow_block'


def rmsnorm(x, g):
    xf = x.astype(jnp.float32)
    y = xf * lax.rsqrt(jnp.mean(xf * xf, axis=-1, keepdims=True) + NORM_EPS)
    return (y * g.astype(jnp.float32)).astype(x.dtype)


def modulate(h, shift, scale):
    return h * (1.0 + scale) + shift


def depthwise_conv(u, w, b, pad_left, pad_right):
    y = lax.conv_general_dilated(u, w[:, None, :], window_strides=(1,), padding=[(pad_left, pad_right)],
                                 dimension_numbers=('NWC', 'WIO', 'NWC'), feature_group_count=u.shape[-1])
    return y + b


def swiglu(h, w_in, w_out):
    gate, up = jnp.split(h @ w_in, 2, axis=-1)
    return (jax.nn.silu(gate) * up) @ w_out


def hyena_filters(L, w1, b1, w2, b2, w3, freq):
    f32 = jnp.float32
    k = jnp.arange(L, dtype=f32)
    t = (k / max(L - 1, 1))[:, None]
    ang = (2.0 * math.pi / L) * k[:, None] * jnp.linspace(1e-4, HY_BANDS - 1, HY_BANDS, dtype=f32)[None, :]
    z = jnp.concatenate([t, jnp.cos(ang), -jnp.sin(ang)], axis=-1)
    freq = freq.astype(f32)
    h = jnp.sin(freq * (z @ w1.astype(f32) + b1.astype(f32)))
    h = jnp.sin(freq * (h @ w2.astype(f32) + b2.astype(f32)))
    h = (h @ w3.astype(f32)).reshape(L, HY_ORDER, D_HY)
    half = L // 2
    dist = jnp.abs(k - half) / max(half, 1)
    deltas = jnp.abs(jnp.linspace(math.log(HY_TARGET) / HY_SLOW_DECAY, math.log(HY_TARGET) / HY_FAST_DECAY,
                                  D_HY, dtype=f32))
    window = jnp.exp(-dist[:, None] * deltas[None, :])
    return jnp.transpose(h * window[:, None, :], (1, 0, 2))


def centred_long_conv(z, h):
    L = z.shape[1]
    n = 2 * L
    y = jnp.fft.irfft(jnp.fft.rfft(z, n=n, axis=1) * jnp.fft.rfft(h, n=n, axis=0)[None], n=n, axis=1)
    return y[:, L // 2: L // 2 + L]


def hyena_mixer(u, short_w, short_b, filt, bias):
    L = u.shape[1]
    uc = depthwise_conv(u, short_w, short_b, HY_SHORT // 2, HY_SHORT // 2).astype(jnp.float32)
    v, x1, x2 = jnp.split(uc, 3, axis=-1)
    h = hyena_filters(L, *filt)
    bias = bias.astype(jnp.float32)
    z = v
    for n, gate in enumerate((x1, x2)):
        z = gate * (centred_long_conv(z, h[n]) + bias[n] * z)
    return z.astype(u.dtype)


def rotate(x, ang):
    n = x.shape[-1] // 2
    cos, sin = jnp.cos(ang), jnp.sin(ang)
    x1, x2 = x[..., :n], x[..., n:]
    return jnp.concatenate([x1 * cos - x2 * sin, x1 * sin + x2 * cos], axis=-1).astype(x.dtype)


def axial_rope(x, ang_row, ang_col):
    half = x.shape[-1] // 2
    return jnp.concatenate([rotate(x[..., :half], ang_row), rotate(x[..., half:], ang_col)], axis=-1)


def mla_queries(u, q_norm_g, w_uq, qk_g_q):
    B, L, _ = u.shape
    q = (rmsnorm(u, q_norm_g) @ w_uq).reshape(B, L, MLA_HEADS, MLA_NOPE + MLA_ROPE)
    return rmsnorm(q[..., :MLA_NOPE], qk_g_q[:MLA_NOPE]), rmsnorm(q[..., MLA_NOPE:], qk_g_q[MLA_NOPE:])


def mla_keys(u, kv_norm_g, w_ukv, qk_g_k):
    B, L, _ = u.shape
    c_kv, k_rope = u[..., :MLA_KV_RANK], u[..., MLA_KV_RANK:]
    kv = (rmsnorm(c_kv, kv_norm_g) @ w_ukv).reshape(B, L, MLA_HEADS, MLA_NOPE + MLA_V)
    k_nope, v = kv[..., :MLA_NOPE], kv[..., MLA_NOPE:]
    return rmsnorm(k_nope, qk_g_k[:MLA_NOPE]), rmsnorm(k_rope, qk_g_k[MLA_NOPE:]), v


def attend(q_nope, q_rope, k_nope, k_rope, v):
    s = (jnp.einsum('bqhn,bkhn->bhqk', q_nope, k_nope, preferred_element_type=jnp.float32)
         + jnp.einsum('bqhr,bkr->bhqk', q_rope, k_rope, preferred_element_type=jnp.float32))
    p = jax.nn.softmax(s * MLA_SCALE, axis=-1)
    return jnp.einsum('bhqk,bkhv->bqhv', p.astype(v.dtype), v)


def attend_blocked(q_nope, q_rope, k_nope, k_rope, v):
    B, L = q_nope.shape[:2]
    nb = L // Q_BLOCK

    def blk(t):
        return jnp.swapaxes(t.reshape((B, nb, Q_BLOCK) + t.shape[2:]), 0, 1)

    out = lax.map(lambda qs: attend(qs[0], qs[1], k_nope, k_rope, v), (blk(q_nope), blk(q_rope)))
    return jnp.swapaxes(out, 0, 1).reshape(B, L, MLA_HEADS * MLA_V)


def even_mixer(h_ctx, h_lat, ang_row, ang_col, w_in, short_w, short_b, filt, f_bias,
               q_norm_g, w_uq, kv_norm_g, w_ukv, qk_g_q, qk_g_k, w_out, need_ctx):
    i_hy = (HY_ORDER + 1) * D_HY
    i_kv = i_hy + MLA_Q_RANK
    u_ctx = h_ctx @ w_in
    u_lat = h_lat @ w_in
    kn_c, kr_c, v_c = mla_keys(u_ctx[..., i_kv:], kv_norm_g, w_ukv, qk_g_k)
    kn_l, kr_l, v_l = mla_keys(u_lat[..., i_kv:], kv_norm_g, w_ukv, qk_g_k)
    kr_l = axial_rope(kr_l, ang_row, ang_col)
    qn_l, qr_l = mla_queries(u_lat[..., i_hy:i_kv], q_norm_g, w_uq, qk_g_q)
    qr_l = axial_rope(qr_l, ang_row[:, None, :], ang_col[:, None, :])
    att_l = attend_blocked(qn_l, qr_l, jnp.concatenate([kn_c, kn_l], axis=1),
                           jnp.concatenate([kr_c, kr_l], axis=1), jnp.concatenate([v_c, v_l], axis=1))
    hy_l = hyena_mixer(u_lat[..., :i_hy], short_w, short_b, filt, f_bias)
    y_lat = jnp.concatenate([hy_l, att_l], axis=-1) @ w_out
    y_ctx = None
    if need_ctx:
        B, C = h_ctx.shape[:2]
        qn_c, qr_c = mla_queries(u_ctx[..., i_hy:i_kv], q_norm_g, w_uq, qk_g_q)
        att_c = attend(qn_c, qr_c, kn_c, kr_c, v_c).reshape(B, C, MLA_HEADS * MLA_V)
        hy_c = hyena_mixer(u_ctx[..., :i_hy], short_w, short_b, filt, f_bias)
        y_ctx = jnp.concatenate([hy_c, att_c], axis=-1) @ w_out
    return y_ctx, y_lat


def rglru_gates(x, w_a, b_a, w_x, b_x, lam):
    B, L, _ = x.shape
    f32 = jnp.float32
    xb = x.reshape(B, L, RG_BLOCKS, RG_BW)
    r = jax.nn.sigmoid(jnp.einsum('blnk,nkj->blnj', xb, w_a.astype(f32)).reshape(B, L, D_RNN) + b_a.astype(f32))
    i = jax.nn.sigmoid(jnp.einsum('blnk,nkj->blnj', xb, w_x.astype(f32)).reshape(B, L, D_RNN) + b_x.astype(f32))
    log_a = -RG_C * r * jax.nn.softplus(-lam.astype(f32))
    return jnp.exp(log_a), jnp.sqrt(-jnp.expm1(2.0 * log_a)) * (i * x)


def linear_scan(a, b, h0=None):
    def combine(left, right):
        return left[0] * right[0], right[0] * left[1] + right[1]

    a_cum, h = lax.associative_scan(combine, (a, b), axis=1)
    return h if h0 is None else h + a_cum * h0[:, None, :]


def odd_mixer(h_ctx, h_lat, w_in, conv_w, conv_b, w_a, b_a, w_x, b_x, lam, w_out, need_ctx):
    def branches(h):
        gate, xr = jnp.split(h @ w_in, 2, axis=-1)
        xr = depthwise_conv(xr, conv_w, conv_b, RG_CONV // 2, RG_CONV - 1 - RG_CONV // 2)
        return gate, xr.astype(jnp.float32)

    g_c, x_c = branches(h_ctx)
    g_l, x_l = branches(h_lat)
    a, b = rglru_gates(x_c, w_a[0], b_a[0], w_x[0], b_x[0], lam[0])
    hc_f = linear_scan(a, b)
    a, b = rglru_gates(x_l, w_a[0], b_a[0], w_x[0], b_x[0], lam[0])
    hl_f = linear_scan(a, b, hc_f[:, -1])
    a, b = rglru_gates(jnp.flip(x_c, 1), w_a[1], b_a[1], w_x[1], b_x[1], lam[1])
    hc_b = linear_scan(a, b)
    a, b = rglru_gates(jnp.flip(x_l, 1), w_a[1], b_a[1], w_x[1], b_x[1], lam[1])
    hl_b = jnp.flip(linear_scan(a, b, hc_b[:, -1]), 1)
    y_lat = (jax.nn.gelu(g_l) * (hl_f + hl_b).astype(g_l.dtype)) @ w_out
    y_ctx = None
    if need_ctx:
        y_ctx = (jax.nn.gelu(g_c) * (hc_f + jnp.flip(hc_b, 1)).astype(g_c.dtype)) @ w_out
    return y_ctx, y_lat


def setup_inputs(seed: int = 0) -> dict:
    key = jax.random.key(seed)
    ks = iter(jax.random.split(key, 48))
    f32 = jnp.float32

    def nrm(shape, scale):
        return jax.random.normal(next(ks), shape, f32) * scale

    def gain(shape):
        return 1.0 + nrm(shape, 0.02)

    D = D_MODEL
    lam_u = jax.random.uniform(next(ks), (N_ODD, 2, D_RNN), f32, 0.9, 0.999)
    lam_a = lam_u ** (1.0 / RG_C)
    rg_lambda = jnp.log(lam_a) - jnp.log1p(-lam_a)
    return {
        'x': nrm((BATCH, SEQ, D), 1.0),
        'c': nrm((BATCH, D), 1.0),
        'ctx': nrm((BATCH, CTX_LEN, D), 1.0),
        'c_ctx': nrm((D,), 1.0),
        'ada_w': nrm((DEPTH, D, 6 * D), 0.02),
        'ada_b': nrm((DEPTH, 6 * D), 0.02),
        'norm1_g': gain((DEPTH, D)),
        'norm2_g': gain((DEPTH, D)),
        'ffn_w_in': nrm((DEPTH, D, 2 * D_FF), D ** -0.5),
        'ffn_w_out': nrm((DEPTH, D_FF, D), D_FF ** -0.5),
        'ev_w_in': nrm((N_EVEN, D, EVEN_IN), D ** -0.5),
        'hy_short_w': nrm((N_EVEN, HY_SHORT, (HY_ORDER + 1) * D_HY), HY_SHORT ** -0.5),
        'hy_short_b': nrm((N_EVEN, (HY_ORDER + 1) * D_HY), 0.02),
        'hy_w1': nrm((N_EVEN, HY_EMB, HY_FILTER_ORDER), HY_EMB ** -0.5),
        'hy_b1': nrm((N_EVEN, HY_FILTER_ORDER), 0.02),
        'hy_w2': nrm((N_EVEN, HY_FILTER_ORDER, HY_FILTER_ORDER), HY_FILTER_ORDER ** -0.5),
        'hy_b2': nrm((N_EVEN, HY_FILTER_ORDER), 0.02),
        'hy_w3': nrm((N_EVEN, HY_FILTER_ORDER, HY_ORDER * D_HY), 0.05 * HY_FILTER_ORDER ** -0.5),
        'hy_freq': gain((N_EVEN, HY_FILTER_ORDER)),
        'hy_bias': nrm((N_EVEN, HY_ORDER, D_HY), 0.5),
        'mla_q_norm_g': gain((N_EVEN, MLA_Q_RANK)),
        'mla_w_uq': nrm((N_EVEN, MLA_Q_RANK, MLA_HEADS * (MLA_NOPE + MLA_ROPE)), MLA_Q_RANK ** -0.5),
        'mla_kv_norm_g': gain((N_EVEN, MLA_KV_RANK)),
        'mla_w_ukv': nrm((N_EVEN, MLA_KV_RANK, MLA_HEADS * (MLA_NOPE + MLA_V)), MLA_KV_RANK ** -0.5),
        'mla_qk_g_q': gain((N_EVEN, MLA_NOPE + MLA_ROPE)),
        'mla_qk_g_k': gain((N_EVEN, MLA_NOPE + MLA_ROPE)),
        'ev_w_out': nrm((N_EVEN, EVEN_MIX, D), EVEN_MIX ** -0.5),
        'rg_w_in': nrm((N_ODD, D, 2 * D_RNN), D ** -0.5),
        'rg_conv_w': nrm((N_ODD, RG_CONV, D_RNN), RG_CONV ** -0.5),
        'rg_conv_b': nrm((N_ODD, D_RNN), 0.02),
        'rg_w_a': nrm((N_ODD, 2, RG_BLOCKS, RG_BW, RG_BW), RG_BW ** -0.5),
        'rg_b_a': nrm((N_ODD, 2, D_RNN), 0.02),
        'rg_w_x': nrm((N_ODD, 2, RG_BLOCKS, RG_BW, RG_BW), RG_BW ** -0.5),
        'rg_b_x': nrm((N_ODD, 2, D_RNN), 0.02),
        'rg_lambda': rg_lambda,
        'rg_w_out': nrm((N_ODD, D_RNN, D), D_RNN ** -0.5),
    }


def reference(x, c, ctx, c_ctx, ada_w, ada_b, norm1_g, norm2_g, ffn_w_in, ffn_w_out,
              ev_w_in, hy_short_w, hy_short_b, hy_w1, hy_b1, hy_w2, hy_b2, hy_w3, hy_freq, hy_bias,
              mla_q_norm_g, mla_w_uq, mla_kv_norm_g, mla_w_ukv, mla_qk_g_q, mla_qk_g_k, ev_w_out,
              rg_w_in, rg_conv_w, rg_conv_b, rg_w_a, rg_b_a, rg_w_x, rg_b_x, rg_lambda, rg_w_out):
    L = x.shape[1]
    rows = L // GRID_W
    row = jnp.repeat(jnp.arange(rows, dtype=jnp.float32), GRID_W)
    col = jnp.tile(jnp.arange(GRID_W, dtype=jnp.float32), rows)
    inv_freq = ROPE_BASE ** (-jnp.arange(0, ROPE_AXIS, 2, dtype=jnp.float32) / ROPE_AXIS)
    ang_row = row[:, None] * inv_freq[None, :]
    ang_col = col[:, None] * inv_freq[None, :]
    s_lat = jax.nn.silu(c)
    s_ctx = jax.nn.silu(c_ctx)
    x_lat, x_ctx = x, ctx
    for i in range(DEPTH):
        j = i // 2
        need_ctx = i < DEPTH - 1
        m_lat = jnp.split((s_lat @ ada_w[i] + ada_b[i])[:, None, :], 6, axis=-1)
        m_ctx = jnp.split(s_ctx @ ada_w[i] + ada_b[i], 6, axis=-1)
        h_lat = modulate(rmsnorm(x_lat, norm1_g[i]), m_lat[0], m_lat[1])
        h_ctx = modulate(rmsnorm(x_ctx, norm1_g[i]), m_ctx[0], m_ctx[1])
        if i % 2 == 0:
            filt = (hy_w1[j], hy_b1[j], hy_w2[j], hy_b2[j], hy_w3[j], hy_freq[j])
            y_ctx, y_lat = even_mixer(h_ctx, h_lat, ang_row, ang_col, ev_w_in[j], hy_short_w[j], hy_short_b[j],
                                      filt, hy_bias[j], mla_q_norm_g[j], mla_w_uq[j], mla_kv_norm_g[j],
                                      mla_w_ukv[j], mla_qk_g_q[j], mla_qk_g_k[j], ev_w_out[j], need_ctx)
        else:
            y_ctx, y_lat = odd_mixer(h_ctx, h_lat, rg_w_in[j], rg_conv_w[j], rg_conv_b[j], rg_w_a[j], rg_b_a[j],
                                     rg_w_x[j], rg_b_x[j], rg_lambda[j], rg_w_out[j], need_ctx)
        x_lat = x_lat + m_lat[2] * y_lat
        x_lat = x_lat + m_lat[5] * swiglu(modulate(rmsnorm(x_lat, norm2_g[i]), m_lat[3], m_lat[4]),
                                          ffn_w_in[i], ffn_w_out[i])
        if need_ctx:
            x_ctx = x_ctx + m_ctx[2] * y_ctx
            x_ctx = x_ctx + m_ctx[5] * swiglu(modulate(rmsnorm(x_ctx, norm2_g[i]), m_ctx[3], m_ctx[4]),
                                              ffn_w_in[i], ffn_w_out[i])
    return x_lat
```

```python
import functools
import math

import jax
import jax.numpy as jnp
from jax import lax
from jax.experimental import pallas as pl
from jax.experimental.pallas import tpu as pltpu

F32 = jnp.float32
BF16 = jnp.bfloat16

D_MODEL = 1024
DEPTH = 4
GRID_W = 64
D_FF = 2816
NORM_EPS = 1e-6

D_HY = 512
HY_BANDS = 16
HY_FAST_DECAY = 0.3
HY_SLOW_DECAY = 1.5
HY_TARGET = 1e-2

MLA_HEADS = 8
MLA_NOPE = 64
MLA_ROPE = 32
MLA_V = 64
MLA_Q_RANK = 384
MLA_KV_RANK = 256
MLA_SCALE = (MLA_NOPE + MLA_ROPE) ** -0.5
ROPE_AXIS = 16
ROPE_BASE = 10000.0
HEAD_LANES = 128

D_RNN = 1280
RG_BLOCKS = 10
RG_BW = 128
RG_CONV = 4
RG_C = 8.0

EVEN_IN_PAD = 3 * D_HY + MLA_Q_RANK + MLA_KV_RANK + HEAD_LANES

VMEM_LIMIT = 56 * 1024 * 1024


def _cparams(*sem):
    return pltpu.CompilerParams(dimension_semantics=sem, vmem_limit_bytes=VMEM_LIMIT)


def _const_spec(shape, single_buffer=False):
    nd = len(shape)
    kw = {"pipeline_mode": pl.Buffered(1)} if single_buffer else {}
    return pl.BlockSpec(shape, lambda *_: (0,) * nd, **kw)


def _ada_kernel(s_ref, w_ref, b_ref, o_ref):
    s = s_ref[...]
    s = s * jax.nn.sigmoid(s)
    o_ref[...] = jnp.dot(s, w_ref[...], preferred_element_type=F32) + b_ref[...]


def ada_modulation(cond8, ada_w, ada_b):
    depth, d, n = ada_w.shape
    tn = 1536
    return pl.pallas_call(
        _ada_kernel,
        grid=(depth, n // tn),
        in_specs=[
            pl.BlockSpec((8, d), lambda i, j: (0, 0)),
            pl.BlockSpec((None, d, tn), lambda i, j: (i, 0, j)),
            pl.BlockSpec((None, 1, tn), lambda i, j: (i, 0, j)),
        ],
        out_specs=pl.BlockSpec((None, 8, tn), lambda i, j: (i, 0, j)),
        out_shape=jax.ShapeDtypeStruct((depth, 8, n), F32),
        compiler_params=_cparams("parallel", "parallel"),
        name="ada_modulation",
    )(cond8, ada_w, ada_b.reshape(depth, 1, n))


def _mod_rows(m_ref, idx, b, is_ctx):
    d = D_MODEL
    lat = m_ref[pl.ds(b, 1), pl.ds(idx * d, d)]
    ctx = m_ref[pl.ds(2, 1), pl.ds(idx * d, d)]
    return jnp.where(is_ctx, ctx, lat)


def _rms(x, g):
    ms = jnp.mean(x * x, axis=-1, keepdims=True)
    return x * lax.rsqrt(ms + NORM_EPS) * g


def _ctx_rows(i, bm, n_lat):
    row = i * bm + lax.broadcasted_iota(jnp.int32, (bm, 1), 0)
    return row >= n_lat


def _even_pre_kernel(n_lat, x_ref, m_ref, g1_ref, w_in_ref, qg_ref, kvg_ref, w_uq_ref, w_uk_ref,
                     w_uv_ref, gq_ref, gk_ref, msq_ref, ct_ref, s1_ref, s2_ref,
                     hy_ref, q_ref, k_ref, v_ref):
    b = pl.program_id(0)
    i = pl.program_id(1)
    bm = x_ref.shape[0]
    is_ctx = _ctx_rows(i, bm, n_lat)
    x = x_ref[...]
    h = _rms(x, g1_ref[...])
    h = h * (1.0 + _mod_rows(m_ref, 1, b, is_ctx)) + _mod_rows(m_ref, 0, b, is_ctx)
    u = jnp.dot(h.astype(BF16), w_in_ref[...], preferred_element_type=F32)
    i_q = 3 * D_HY
    i_kv = i_q + MLA_Q_RANK
    i_kr = i_kv + MLA_KV_RANK
    hy_ref[...] = u[:, :i_q].astype(BF16)

    ct = ct_ref[...]
    s1 = s1_ref[...]
    s2 = s2_ref[...]
    msq = msq_ref[...]

    def head_norm(t, g):
        ms = jnp.dot((t * t).astype(BF16), msq, preferred_element_type=F32)
        return t * lax.rsqrt(ms + NORM_EPS) * g

    def rope(t):
        return (t * ct + pltpu.roll(t, HEAD_LANES - ROPE_AXIS // 2, 1) * s1
                + pltpu.roll(t, ROPE_AXIS // 2, 1) * s2)

    qn = _rms(u[:, i_q:i_kv], qg_ref[...]).astype(BF16)
    kvn = _rms(u[:, i_kv:i_kr], kvg_ref[...]).astype(BF16)
    gq = gq_ref[...]
    gk = gk_ref[...]
    k_rope = rope(head_norm(u[:, i_kr:], gk))
    for hd in range(MLA_HEADS):
        sl = slice(hd * HEAD_LANES, (hd + 1) * HEAD_LANES)
        qh = jnp.dot(qn, w_uq_ref[:, sl], preferred_element_type=F32)
        q_ref[hd] = rope(head_norm(qh, gq)).astype(BF16)
        kh = jnp.dot(kvn, w_uk_ref[:, sl], preferred_element_type=F32)
        k_ref[hd] = (head_norm(kh, gk) + k_rope).astype(BF16)
    v = jnp.dot(kvn, w_uv_ref[...], preferred_element_type=F32)
    for p in range(MLA_HEADS // 2):
        v_ref[p] = v[:, p * HEAD_LANES:(p + 1) * HEAD_LANES].astype(BF16)


def even_pre(x_all, m, g1, w, tabs, n_lat, bm):
    B, T, D = x_all.shape
    H = MLA_HEADS
    row = lambda n: pl.BlockSpec((None, bm, n), lambda b, i: (b, i, 0))
    tab = pl.BlockSpec((bm, HEAD_LANES), lambda b, i: (i, 0))
    headed = lambda nh: pl.BlockSpec((None, nh, bm, HEAD_LANES), lambda b, i: (b, 0, i, 0))
    consts = [m, g1, w["w_in"], w["q_norm_g"], w["kv_norm_g"], w["w_uq"], w["w_uk"], w["w_uv"],
              w["gq"], w["gk"], tabs["msq"]]
    return pl.pallas_call(
        functools.partial(_even_pre_kernel, n_lat),
        grid=(B, T // bm),
        in_specs=[row(D)] + [_const_spec(c.shape) for c in consts] + [tab, tab, tab],
        out_specs=[row(3 * D_HY), headed(H), headed(H), headed(H // 2)],
        out_shape=[
            jax.ShapeDtypeStruct((B, T, 3 * D_HY), BF16),
            jax.ShapeDtypeStruct((B, H, T, HEAD_LANES), BF16),
            jax.ShapeDtypeStruct((B, H, T, HEAD_LANES), BF16),
            jax.ShapeDtypeStruct((B, H // 2, T, HEAD_LANES), BF16),
        ],
        compiler_params=_cparams("parallel", "parallel"),
        name="even_pre",
    )(x_all, *consts, tabs["ct"], tabs["s1"], tabs["s2"])


def _attn_kernel(tk, q_ref, k_ref, v_ref, o_ref):
    tq = q_ref.shape[1]
    n_chunks = k_ref.shape[1] // tk
    outs = []
    for hh in range(2):
        q = q_ref[hh]

        def body(c, carry):
            m, l, acc = carry
            ks = pl.multiple_of(c * tk, tk)
            kc = k_ref[hh, pl.ds(ks, tk), :]
            vc = v_ref[pl.ds(ks, tk), :]
            s = lax.dot_general(q, kc, (((1,), (1,)), ((), ())), preferred_element_type=F32)
            m_new = jnp.maximum(m, jnp.max(s, axis=-1, keepdims=True))
            alpha = jnp.exp(m - m_new)
            p = jnp.exp(s - m_new)
            l = alpha * l + jnp.sum(p, axis=-1, keepdims=True)
            acc = alpha * acc + jnp.dot(p.astype(BF16), vc, preferred_element_type=F32)
            return m_new, l, acc

        init = (jnp.full((tq, 1), -jnp.inf, F32), jnp.zeros((tq, 1), F32),
                jnp.zeros((tq, HEAD_LANES), F32))
        m, l, acc = lax.fori_loop(0, n_chunks, body, init)
        outs.append(acc / l)
    lane = lax.broadcasted_iota(jnp.int32, (tq, HEAD_LANES), 1)
    o_ref[...] = jnp.where(lane < MLA_V, outs[0], outs[1]).astype(o_ref.dtype)


def attention(q, k, v, q_blk0, n_q, tq, k_blk0, n_k, tk):
    B, H, T, _ = q.shape
    return pl.pallas_call(
        functools.partial(_attn_kernel, tk),
        grid=(B, H // 2, n_q // tq),
        in_specs=[
            pl.BlockSpec((None, 2, tq, HEAD_LANES), lambda b, h, i: (b, h, q_blk0 + i, 0)),
            pl.BlockSpec((None, 2, n_k, HEAD_LANES), lambda b, h, i: (b, h, k_blk0, 0)),
            pl.BlockSpec((None, None, n_k, HEAD_LANES), lambda b, h, i: (b, h, k_blk0, 0)),
        ],
        out_specs=pl.BlockSpec((None, tq, HEAD_LANES), lambda b, h, i: (b, i, h)),
        out_shape=jax.ShapeDtypeStruct((B, n_q, H * MLA_V), BF16),
        compiler_params=_cparams("parallel", "parallel", "parallel"),
        name="mla_attention",
    )(q, k, v)


def _post_kernel(n_mix, n_lat, tf, *refs):
    x_ref, m_ref, g2_ref = refs[:3]
    mix = refs[3:3 + 2 * n_mix]
    wg_ref, wu_ref, wd_ref, o_ref = refs[3 + 2 * n_mix:]
    b = pl.program_id(0)
    i = pl.program_id(1)
    bm = x_ref.shape[0]
    is_ctx = _ctx_rows(i, bm, n_lat)
    y = None
    for j in range(n_mix):
        t = jnp.dot(mix[2 * j][...], mix[2 * j + 1][...], preferred_element_type=F32)
        y = t if y is None else y + t
    x1 = x_ref[...] + _mod_rows(m_ref, 2, b, is_ctx) * y
    h = _rms(x1, g2_ref[...])
    h = (h * (1.0 + _mod_rows(m_ref, 4, b, is_ctx)) + _mod_rows(m_ref, 3, b, is_ctx)).astype(BF16)
    acc = None
    for f0 in range(0, D_FF, tf):
        g = jnp.dot(h, wg_ref[:, f0:f0 + tf], preferred_element_type=F32)
        u = jnp.dot(h, wu_ref[:, f0:f0 + tf], preferred_element_type=F32)
        a = (g * jax.nn.sigmoid(g) * u).astype(BF16)
        t = jnp.dot(a, wd_ref[f0:f0 + tf, :], preferred_element_type=F32)
        acc = t if acc is None else acc + t
    o_ref[...] = x1 + _mod_rows(m_ref, 5, b, is_ctx) * acc


def post_mixer(x_all, m, g2, mixes, wg, wu, wd, n_lat, n_rows, bm, tf=1408):
    B, T, D = x_all.shape
    row = lambda n: pl.BlockSpec((None, bm, n), lambda b, i: (b, i, 0))
    mix_args, mix_specs = [], []
    for a, wmat in mixes:
        mix_args += [a, wmat]
        mix_specs += [row(a.shape[-1]), _const_spec(wmat.shape, True)]
    return pl.pallas_call(
        functools.partial(_post_kernel, len(mixes), n_lat, tf),
        grid=(B, n_rows // bm),
        in_specs=[row(D), _const_spec(m.shape), _const_spec(g2.shape)] + mix_specs
        + [_const_spec(wg.shape, True), _const_spec(wu.shape, True), _const_spec(wd.shape, True)],
        out_specs=row(D),
        out_shape=jax.ShapeDtypeStruct((B, n_rows, D), F32),
        compiler_params=_cparams("parallel", "parallel"),
        name="post_mixer",
    )(x_all, m, g2, *mix_args, wg, wu, wd)


def _odd_pre_kernel(n_lat, x_ref, m_ref, g1_ref, w_ref, gate_ref, xr_ref):
    b = pl.program_id(0)
    i = pl.program_id(1)
    bm = x_ref.shape[0]
    is_ctx = _ctx_rows(i, bm, n_lat)
    h = _rms(x_ref[...], g1_ref[...])
    h = h * (1.0 + _mod_rows(m_ref, 1, b, is_ctx)) + _mod_rows(m_ref, 0, b, is_ctx)
    u = jnp.dot(h.astype(BF16), w_ref[...], preferred_element_type=F32)
    gate_ref[...] = u[:, :D_RNN].astype(BF16)
    xr_ref[...] = u[:, D_RNN:]


def odd_pre(x_all, m, g1, w_in, n_lat, bm):
    B, T, D = x_all.shape
    row = lambda n: pl.BlockSpec((None, bm, n), lambda b, i: (b, i, 0))
    return pl.pallas_call(
        functools.partial(_odd_pre_kernel, n_lat),
        grid=(B, T // bm),
        in_specs=[row(D), _const_spec(m.shape), _const_spec(g1.shape), _const_spec(w_in.shape)],
        out_specs=[row(D_RNN), row(D_RNN)],
        out_shape=[jax.ShapeDtypeStruct((B, T, D_RNN), BF16), jax.ShapeDtypeStruct((B, T, D_RNN), F32)],
        compiler_params=_cparams("parallel", "parallel"),
        name="odd_pre",
    )(x_all, m, g1, w_in)


def _rglru_kernel(n_lat, chunk, gate_ref, xr_ref, cw_ref, cb_ref, wg_ref, bg_ref, lam_ref, o_ref, hf_ref):
    T = xr_ref.shape[0]
    n_ctx = T - n_lat
    R = chunk
    nv = R // 8
    cw = cw_ref[...]
    cb = cb_ref[...]
    sub = lax.broadcasted_iota(jnp.int32, (nv, 8, RG_BW), 1)

    def conv_chunk(c0, seg_lo, seg_hi):
        lo = jnp.maximum(c0 - 8, 0)
        hi = jnp.minimum(c0 + R, T - 8)
        win = jnp.concatenate([xr_ref[pl.ds(pl.multiple_of(lo, 8), 8), :],
                               xr_ref[pl.ds(pl.multiple_of(c0, 8), R), :],
                               xr_ref[pl.ds(pl.multiple_of(hi, 8), 8), :]], axis=0)
        g = c0 - 8 + lax.broadcasted_iota(jnp.int32, (R + 16, 1), 0)
        win = jnp.where((g >= seg_lo) & (g < seg_hi), win, 0.0)
        y = cb
        for kk in range(RG_CONV):
            off = 8 + kk - RG_CONV // 2
            y = y + win[off:off + R, :] * cw[kk:kk + 1, :]
        return y

    def gates(xc, d):
        pre = jnp.dot(xc.astype(BF16), wg_ref[d], preferred_element_type=F32) + bg_ref[d]
        r = jax.nn.sigmoid(pre[:, :RG_BW])
        ig = jax.nn.sigmoid(pre[:, RG_BW:])
        log_a = -RG_C * r * jax.nn.softplus(-lam_ref[d])
        a = jnp.exp(log_a)
        bt = jnp.sqrt(-jnp.tanh(log_a) * (a * a + 1.0)) * (ig * xc)
        return a, bt

    def local_scan(a, bt, reverse):
        a = a.reshape(nv, 8, RG_BW)
        bt = bt.reshape(nv, 8, RG_BW)
        for d in (1, 2, 4):
            if reverse:
                keep = sub < 8 - d
                sh = 8 - d
            else:
                keep = sub >= d
                sh = d
            a_s = jnp.where(keep, pltpu.roll(a, sh, 1), 1.0)
            b_s = jnp.where(keep, pltpu.roll(bt, sh, 1), 0.0)
            bt = a * b_s + bt
            a = a * a_s
        return a, bt

    def chunk_scan(a, bt, h, reverse):
        a, bt = local_scan(a, bt, reverse)
        rows = [None] * nv
        order = range(nv - 1, -1, -1) if reverse else range(nv)
        edge = 0 if reverse else 7
        for v in order:
            hv = bt[v] + a[v] * h
            rows[v] = hv
            h = hv[edge:edge + 1, :]
        return jnp.concatenate(rows, axis=0), h

    n_lat_chunks = n_lat // R
    h0 = jnp.zeros((1, RG_BW), F32)

    def fwd_chunk(c0, seg_lo, seg_hi, h):
        xc = conv_chunk(c0, seg_lo, seg_hi)
        a, bt = gates(xc, 0)
        hs, h = chunk_scan(a, bt, h, False)
        hf_ref[pl.ds(pl.multiple_of(c0, 8), R), :] = hs
        return h

    h = h0
    for c in range(n_ctx // R):
        h = fwd_chunk(n_lat + c * R, n_lat, T, h)
    h = lax.fori_loop(0, n_lat_chunks, lambda c, hh: fwd_chunk(c * R, 0, n_lat, hh), h)

    def bwd_chunk(c0, seg_lo, seg_hi, h):
        xc = conv_chunk(c0, seg_lo, seg_hi)
        a, bt = gates(xc, 1)
        hs, h = chunk_scan(a, bt, h, True)
        rs = pl.ds(pl.multiple_of(c0, 8), R)
        g = gate_ref[rs, :].astype(F32)
        o_ref[rs, :] = (jax.nn.gelu(g) * (hf_ref[rs, :] + hs)).astype(o_ref.dtype)
        return h

    h = h0
    for c in range(n_ctx // R - 1, -1, -1):
        h = bwd_chunk(n_lat + c * R, n_lat, T, h)
    lax.fori_loop(0, n_lat_chunks, lambda c, hh: bwd_chunk((n_lat_chunks - 1 - c) * R, 0, n_lat, hh), h)


def rglru(gate, xr, cw, cb, wg, bg, lam, n_lat, chunk=256):
    B, T, _ = xr.shape
    col = pl.BlockSpec((None, T, RG_BW), lambda b, n: (b, 0, n))
    return pl.pallas_call(
        functools.partial(_rglru_kernel, n_lat, chunk),
        grid=(B, RG_BLOCKS),
        in_specs=[
            col, col,
            pl.BlockSpec((RG_CONV, RG_BW), lambda b, n: (0, n)),
            pl.BlockSpec((1, RG_BW), lambda b, n: (0, n)),
            pl.BlockSpec((None, 2, RG_BW, 2 * RG_BW), lambda b, n: (n, 0, 0, 0)),
            pl.BlockSpec((None, 2, 1, 2 * RG_BW), lambda b, n: (n, 0, 0, 0)),
            pl.BlockSpec((None, 2, 1, RG_BW), lambda b, n: (n, 0, 0, 0)),
        ],
        out_specs=col,
        out_shape=jax.ShapeDtypeStruct((B, T, D_RNN), BF16),
        scratch_shapes=[pltpu.VMEM((T, RG_BW), F32)],
        compiler_params=_cparams("parallel", "parallel"),
        name="rglru",
    )(gate, xr, cw, cb, wg, bg, lam)


def _hyena_filters(L, w1, b1, w2, b2, w3, freq):
    k = jnp.arange(L, dtype=F32)
    t = (k / max(L - 1, 1))[:, None]
    ang = (2.0 * math.pi / L) * k[:, None] * jnp.linspace(1e-4, HY_BANDS - 1, HY_BANDS, dtype=F32)[None, :]
    z = jnp.concatenate([t, jnp.cos(ang), -jnp.sin(ang)], axis=-1)
    h = jnp.sin(freq * (z @ w1 + b1))
    h = jnp.sin(freq * (h @ w2 + b2))
    h = (h @ w3).reshape(L, 2, D_HY)
    half = L // 2
    dist = jnp.abs(k - half) / max(half, 1)
    deltas = jnp.abs(jnp.linspace(math.log(HY_TARGET) / HY_SLOW_DECAY, math.log(HY_TARGET) / HY_FAST_DECAY,
                                  D_HY, dtype=F32))
    window = jnp.exp(-dist[:, None] * deltas[None, :])
    return jnp.transpose(h * window[:, None, :], (1, 0, 2))


def _hyena_jax(u, short_w, short_b, filt, bias):
    L = u.shape[1]
    u = u.astype(F32)
    up = jnp.pad(u, ((0, 0), (1, 1), (0, 0)))
    uc = up[:, :-2] * short_w[0] + up[:, 1:-1] * short_w[1] + up[:, 2:] * short_w[2] + short_b
    v, x1, x2 = jnp.split(uc, 3, axis=-1)
    h = _hyena_filters(L, *filt)
    n = 2 * L
    z = v
    for o, gate in enumerate((x1, x2)):
        y = jnp.fft.irfft(jnp.fft.rfft(z, n=n, axis=1) * jnp.fft.rfft(h[o], n=n, axis=0)[None], n=n, axis=1)
        z = gate * (y[:, L // 2:L // 2 + L] + bias[o] * z)
    return z.astype(BF16)


def _rope_tables(n_lat, n_ctx):
    rows = n_lat // GRID_W
    row = jnp.repeat(jnp.arange(rows, dtype=F32), GRID_W)
    col = jnp.tile(jnp.arange(GRID_W, dtype=F32), rows)
    inv_freq = ROPE_BASE ** (-jnp.arange(0, ROPE_AXIS, 2, dtype=F32) / ROPE_AXIS)
    ang = jnp.concatenate([row[:, None] * inv_freq[None, :], col[:, None] * inv_freq[None, :]], axis=-1)
    ang = jnp.concatenate([ang, jnp.zeros((n_ctx, ROPE_AXIS), F32)], axis=0)
    cos, sin = jnp.cos(ang), jnp.sin(ang)
    T = n_lat + n_ctx
    z8 = jnp.zeros((T, ROPE_AXIS // 2), F32)
    cos32 = jnp.concatenate([cos[:, :8], cos[:, :8], cos[:, 8:], cos[:, 8:]], axis=-1)
    first = jnp.concatenate([-sin[:, :8], z8, -sin[:, 8:], z8], axis=-1)
    second = jnp.concatenate([z8, sin[:, :8], z8, sin[:, 8:]], axis=-1)
    pad = lambda t, fill: jnp.concatenate(
        [jnp.full((T, MLA_NOPE), fill, F32), t, jnp.zeros((T, HEAD_LANES - MLA_NOPE - MLA_ROPE), F32)], axis=-1)
    lane = jnp.arange(HEAD_LANES)
    seg = jnp.where(lane < MLA_NOPE, 0, jnp.where(lane < MLA_NOPE + MLA_ROPE, 1, 2))
    msq = (seg[:, None] == seg[None, :]) & (seg[:, None] < 2)
    msq = jnp.where(msq, jnp.where(seg[:, None] == 0, 1.0 / MLA_NOPE, 1.0 / MLA_ROPE), 0.0).astype(BF16)
    return {"ct": pad(cos32, 1.0), "s1": pad(first, 0.0), "s2": pad(second, 0.0), "msq": msq}


def _head_gain(g, scale):
    return jnp.concatenate([g * scale, jnp.zeros((HEAD_LANES - g.shape[0],), F32)])[None, :]


def _even_weights(j, ev_w_in, mla_q_norm_g, mla_w_uq, mla_kv_norm_g, mla_w_ukv, mla_qk_g_q, mla_qk_g_k):
    i_q = 3 * D_HY
    i_kv = i_q + MLA_Q_RANK
    i_kr = i_kv + MLA_KV_RANK
    w_in = ev_w_in[j]
    d = w_in.shape[0]
    kr_group = jnp.concatenate([jnp.zeros((d, MLA_NOPE), F32), w_in[:, i_kr:],
                                jnp.zeros((d, HEAD_LANES - MLA_NOPE - MLA_ROPE), F32)], axis=-1)
    w_in_p = jnp.concatenate([w_in[:, :i_kr], kr_group], axis=-1).astype(BF16)
    hq = MLA_NOPE + MLA_ROPE
    w_uq = mla_w_uq[j].reshape(MLA_Q_RANK, MLA_HEADS, hq)
    w_uq = jnp.pad(w_uq, ((0, 0), (0, 0), (0, HEAD_LANES - hq))).reshape(MLA_Q_RANK, -1).astype(BF16)
    w_ukv = mla_w_ukv[j].reshape(MLA_KV_RANK, MLA_HEADS, MLA_NOPE + MLA_V)
    w_uk = jnp.pad(w_ukv[:, :, :MLA_NOPE], ((0, 0), (0, 0), (0, HEAD_LANES - MLA_NOPE)))
    w_uk = w_uk.reshape(MLA_KV_RANK, -1).astype(BF16)
    w_uv = w_ukv[:, :, MLA_NOPE:].reshape(MLA_KV_RANK, -1).astype(BF16)
    return {
        "w_in": w_in_p, "q_norm_g": mla_q_norm_g[j][None, :], "kv_norm_g": mla_kv_norm_g[j][None, :],
        "w_uq": w_uq, "w_uk": w_uk, "w_uv": w_uv,
        "gq": _head_gain(mla_qk_g_q[j], MLA_SCALE), "gk": _head_gain(mla_qk_g_k[j], 1.0),
    }


def _rg_gate_weights(j, rg_w_a, rg_b_a, rg_w_x, rg_b_x, rg_lambda):
    wg = jnp.concatenate([rg_w_a[j], rg_w_x[j]], axis=-1)
    wg = jnp.transpose(wg, (1, 0, 2, 3)).astype(BF16)
    blk = lambda t: jnp.transpose(t.reshape(2, RG_BLOCKS, RG_BW), (1, 0, 2))
    bg = jnp.concatenate([blk(rg_b_a[j]), blk(rg_b_x[j])], axis=-1)[:, :, None, :]
    lam = blk(rg_lambda[j])[:, :, None, :]
    return wg, bg, lam


def kernel(x, c, ctx, c_ctx, ada_w, ada_b, norm1_g, norm2_g, ffn_w_in, ffn_w_out, ev_w_in, hy_short_w, hy_short_b, hy_w1, hy_b1, hy_w2, hy_b2, hy_w3, hy_freq, hy_bias, mla_q_norm_g, mla_w_uq, mla_kv_norm_g, mla_w_ukv, mla_qk_g_q, mla_qk_g_k, ev_w_out, rg_w_in, rg_conv_w, rg_conv_b, rg_w_a, rg_b_a, rg_w_x, rg_b_x, rg_lambda, rg_w_out):
    B, n_lat, D = x.shape
    n_ctx = ctx.shape[1]
    T = n_lat + n_ctx
    bm = 768
    assert T % bm == 0 and n_lat % 512 == 0 and n_ctx == 256

    cond8 = jnp.concatenate([c, c_ctx[None, :], jnp.zeros((8 - B - 1, D), F32)], axis=0)
    mods = ada_modulation(cond8, ada_w, ada_b)
    tabs = _rope_tables(n_lat, n_ctx)
    x_all = jnp.concatenate([x, ctx], axis=1)

    for i in range(DEPTH):
        j = i // 2
        last = i == DEPTH - 1
        m = mods[i]
        g1 = norm1_g[i][None, :]
        g2 = norm2_g[i][None, :]
        if i % 2 == 0:
            w = _even_weights(j, ev_w_in, mla_q_norm_g, mla_w_uq, mla_kv_norm_g, mla_w_ukv, mla_qk_g_q, mla_qk_g_k)
            u_hy, q, k, v = even_pre(x_all, m, g1, w, tabs, n_lat, bm)
            att_l = attention(q, k, v, 0, n_lat, 512, 0, T, 768)
            att_c = attention(q, k, v, n_lat // n_ctx, n_ctx, n_ctx, n_lat // n_ctx, n_ctx, n_ctx)
            att = jnp.concatenate([att_l, att_c], axis=1)
            filt = (hy_w1[j], hy_b1[j], hy_w2[j], hy_b2[j], hy_w3[j], hy_freq[j])
            hy = jnp.concatenate([
                _hyena_jax(u_hy[:, :n_lat], hy_short_w[j], hy_short_b[j], filt, hy_bias[j]),
                _hyena_jax(u_hy[:, n_lat:], hy_short_w[j], hy_short_b[j], filt, hy_bias[j])], axis=1)
            w_out = ev_w_out[j].astype(BF16)
            mixes = [(hy, w_out[:D_HY]), (att, w_out[D_HY:])]
        else:
            gate, xr = odd_pre(x_all, m, g1, rg_w_in[j].astype(BF16), n_lat, bm)
            wg, bg, lam = _rg_gate_weights(j, rg_w_a, rg_b_a, rg_w_x, rg_b_x, rg_lambda)
            y = rglru(gate, xr, rg_conv_w[j], rg_conv_b[j][None, :], wg, bg, lam, n_lat)
            mixes = [(y, rg_w_out[j].astype(BF16))]
        wg_f = ffn_w_in[i][:, :D_FF].astype(BF16)
        wu_f = ffn_w_in[i][:, D_FF:].astype(BF16)
        wd_f = ffn_w_out[i].astype(BF16)
        if last:
            x_all = post_mixer(x_all, m, g2, mixes, wg_f, wu_f, wd_f, n_lat, n_lat, 512)
        else:
            x_all = post_mixer(x_all, m, g2, mixes, wg_f, wu_f, wd_f, n_lat, T, bm)
    return x_all
```

```python
import functools
import math

import jax
import jax.numpy as jnp
from jax import lax
from jax.experimental import pallas as pl
from jax.experimental.pallas import tpu as pltpu

F32 = jnp.float32
BF16 = jnp.bfloat16

D_MODEL = 1024
DEPTH = 4
GRID_W = 64
D_FF = 2816
NORM_EPS = 1e-6

D_HY = 512
HY_CB = 128
HY_NCB = D_HY // HY_CB
HY_N = 128
HY_BANDS = 16
HY_FAST_DECAY = 0.3
HY_SLOW_DECAY = 1.5
HY_TARGET = 1e-2

MLA_HEADS = 8
MLA_NOPE = 64
MLA_ROPE = 32
MLA_V = 64
MLA_Q_RANK = 384
MLA_KV_RANK = 256
MLA_SCALE = (MLA_NOPE + MLA_ROPE) ** -0.5
ROPE_AXIS = 16
ROPE_BASE = 10000.0
HEAD_LANES = 128

D_RNN = 1280
RG_BLOCKS = 10
RG_BW = 128
RG_CONV = 4
RG_C = 8.0

EVEN_IN_PAD = 3 * D_HY + MLA_Q_RANK + MLA_KV_RANK + HEAD_LANES

VMEM_LIMIT = 56 * 1024 * 1024


def _cparams(*sem):
    return pltpu.CompilerParams(dimension_semantics=sem, vmem_limit_bytes=VMEM_LIMIT)


def _const_spec(shape, single_buffer=False):
    nd = len(shape)
    kw = {"pipeline_mode": pl.Buffered(1)} if single_buffer else {}
    return pl.BlockSpec(shape, lambda *_: (0,) * nd, **kw)


def _ada_kernel(s_ref, w_ref, b_ref, o_ref):
    s = s_ref[...]
    s = s * jax.nn.sigmoid(s)
    o_ref[...] = jnp.dot(s, w_ref[...], preferred_element_type=F32) + b_ref[...]


def ada_modulation(cond8, ada_w, ada_b):
    depth, d, n = ada_w.shape
    tn = 1536
    return pl.pallas_call(
        _ada_kernel,
        grid=(depth, n // tn),
        in_specs=[
            pl.BlockSpec((8, d), lambda i, j: (0, 0)),
            pl.BlockSpec((None, d, tn), lambda i, j: (i, 0, j)),
            pl.BlockSpec((None, 1, tn), lambda i, j: (i, 0, j)),
        ],
        out_specs=pl.BlockSpec((None, 8, tn), lambda i, j: (i, 0, j)),
        out_shape=jax.ShapeDtypeStruct((depth, 8, n), F32),
        compiler_params=_cparams("parallel", "parallel"),
        name="ada_modulation",
    )(cond8, ada_w, ada_b.reshape(depth, 1, n))


def _mod_rows(m_ref, idx, b, is_ctx):
    d = D_MODEL
    lat = m_ref[pl.ds(b, 1), pl.ds(idx * d, d)]
    ctx = m_ref[pl.ds(2, 1), pl.ds(idx * d, d)]
    return jnp.where(is_ctx, ctx, lat)


def _rms(x, g):
    ms = jnp.mean(x * x, axis=-1, keepdims=True)
    return x * lax.rsqrt(ms + NORM_EPS) * g


def _ctx_rows(i, bm, n_lat):
    row = i * bm + lax.broadcasted_iota(jnp.int32, (bm, 1), 0)
    return row >= n_lat


def _even_pre_kernel(n_lat, x_ref, m_ref, g1_ref, w_in_ref, qg_ref, kvg_ref, w_uq_ref, w_uk_ref,
                     w_uv_ref, gq_ref, gk_ref, msq_ref, ct_ref, s1_ref, s2_ref,
                     hy_ref, q_ref, k_ref, v_ref):
    b = pl.program_id(0)
    i = pl.program_id(1)
    bm = x_ref.shape[0]
    is_ctx = _ctx_rows(i, bm, n_lat)
    x = x_ref[...]
    h = _rms(x, g1_ref[...])
    h = h * (1.0 + _mod_rows(m_ref, 1, b, is_ctx)) + _mod_rows(m_ref, 0, b, is_ctx)
    u = jnp.dot(h.astype(BF16), w_in_ref[...], preferred_element_type=F32)
    i_q = 3 * D_HY
    i_kv = i_q + MLA_Q_RANK
    i_kr = i_kv + MLA_KV_RANK
    hy_ref[...] = u[:, :i_q].astype(BF16)

    ct = ct_ref[...]
    s1 = s1_ref[...]
    s2 = s2_ref[...]
    msq = msq_ref[...]

    def head_norm(t, g):
        ms = jnp.dot((t * t).astype(BF16), msq, preferred_element_type=F32)
        return t * lax.rsqrt(ms + NORM_EPS) * g

    def rope(t):
        return (t * ct + pltpu.roll(t, HEAD_LANES - ROPE_AXIS // 2, 1) * s1
                + pltpu.roll(t, ROPE_AXIS // 2, 1) * s2)

    qn = _rms(u[:, i_q:i_kv], qg_ref[...]).astype(BF16)
    kvn = _rms(u[:, i_kv:i_kr], kvg_ref[...]).astype(BF16)
    gq = gq_ref[...]
    gk = gk_ref[...]
    k_rope = rope(head_norm(u[:, i_kr:], gk))
    for hd in range(MLA_HEADS):
        sl = slice(hd * HEAD_LANES, (hd + 1) * HEAD_LANES)
        qh = jnp.dot(qn, w_uq_ref[:, sl], preferred_element_type=F32)
        q_ref[hd] = rope(head_norm(qh, gq)).astype(BF16)
        kh = jnp.dot(kvn, w_uk_ref[:, sl], preferred_element_type=F32)
        k_ref[hd] = (head_norm(kh, gk) + k_rope).astype(BF16)
    v = jnp.dot(kvn, w_uv_ref[...], preferred_element_type=F32)
    for p in range(MLA_HEADS // 2):
        v_ref[p] = v[:, p * HEAD_LANES:(p + 1) * HEAD_LANES].astype(BF16)


def even_pre(x_all, m, g1, w, tabs, n_lat, bm):
    B, T, D = x_all.shape
    H = MLA_HEADS
    row = lambda n: pl.BlockSpec((None, bm, n), lambda b, i: (b, i, 0))
    tab = pl.BlockSpec((bm, HEAD_LANES), lambda b, i: (i, 0))
    headed = lambda nh: pl.BlockSpec((None, nh, bm, HEAD_LANES), lambda b, i: (b, 0, i, 0))
    consts = [m, g1, w["w_in"], w["q_norm_g"], w["kv_norm_g"], w["w_uq"], w["w_uk"], w["w_uv"],
              w["gq"], w["gk"], tabs["msq"]]
    return pl.pallas_call(
        functools.partial(_even_pre_kernel, n_lat),
        grid=(B, T // bm),
        in_specs=[row(D)] + [_const_spec(c.shape) for c in consts] + [tab, tab, tab],
        out_specs=[row(3 * D_HY), headed(H), headed(H), headed(H // 2)],
        out_shape=[
            jax.ShapeDtypeStruct((B, T, 3 * D_HY), BF16),
            jax.ShapeDtypeStruct((B, H, T, HEAD_LANES), BF16),
            jax.ShapeDtypeStruct((B, H, T, HEAD_LANES), BF16),
            jax.ShapeDtypeStruct((B, H // 2, T, HEAD_LANES), BF16),
        ],
        compiler_params=_cparams("parallel", "parallel"),
        name="even_pre",
    )(x_all, *consts, tabs["ct"], tabs["s1"], tabs["s2"])


def _attn_kernel(tk, q_ref, k_ref, v_ref, o_ref):
    tq = q_ref.shape[1]
    n_chunks = k_ref.shape[1] // tk
    outs = []
    for hh in range(2):
        q = q_ref[hh]

        def body(c, carry):
            m, l, acc = carry
            ks = pl.multiple_of(c * tk, tk)
            kc = k_ref[hh, pl.ds(ks, tk), :]
            vc = v_ref[pl.ds(ks, tk), :]
            s = lax.dot_general(q, kc, (((1,), (1,)), ((), ())), preferred_element_type=F32)
            m_new = jnp.maximum(m, jnp.max(s, axis=-1, keepdims=True))
            alpha = jnp.exp(m - m_new)
            p = jnp.exp(s - m_new)
            l = alpha * l + jnp.sum(p, axis=-1, keepdims=True)
            acc = alpha * acc + jnp.dot(p.astype(BF16), vc, preferred_element_type=F32)
            return m_new, l, acc

        init = (jnp.full((tq, 1), -jnp.inf, F32), jnp.zeros((tq, 1), F32),
                jnp.zeros((tq, HEAD_LANES), F32))
        m, l, acc = lax.fori_loop(0, n_chunks, body, init)
        outs.append(acc / l)
    lane = lax.broadcasted_iota(jnp.int32, (tq, HEAD_LANES), 1)
    o_ref[...] = jnp.where(lane < MLA_V, outs[0], outs[1]).astype(o_ref.dtype)


def attention(q, k, v, q_blk0, n_q, tq, k_blk0, n_k, tk):
    B, H, T, _ = q.shape
    return pl.pallas_call(
        functools.partial(_attn_kernel, tk),
        grid=(B, H // 2, n_q // tq),
        in_specs=[
            pl.BlockSpec((None, 2, tq, HEAD_LANES), lambda b, h, i: (b, h, q_blk0 + i, 0)),
            pl.BlockSpec((None, 2, n_k, HEAD_LANES), lambda b, h, i: (b, h, k_blk0, 0)),
            pl.BlockSpec((None, None, n_k, HEAD_LANES), lambda b, h, i: (b, h, k_blk0, 0)),
        ],
        out_specs=pl.BlockSpec((None, tq, HEAD_LANES), lambda b, h, i: (b, i, h)),
        out_shape=jax.ShapeDtypeStruct((B, n_q, H * MLA_V), BF16),
        compiler_params=_cparams("parallel", "parallel", "parallel"),
        name="mla_attention",
    )(q, k, v)


def _post_kernel(n_mix, n_lat, tf, *refs):
    x_ref, m_ref, g2_ref = refs[:3]
    mix = refs[3:3 + 2 * n_mix]
    wg_ref, wu_ref, wd_ref, o_ref = refs[3 + 2 * n_mix:]
    b = pl.program_id(0)
    i = pl.program_id(1)
    bm = x_ref.shape[0]
    is_ctx = _ctx_rows(i, bm, n_lat)
    y = None
    for j in range(n_mix):
        t = jnp.dot(mix[2 * j][...], mix[2 * j + 1][...], preferred_element_type=F32)
        y = t if y is None else y + t
    x1 = x_ref[...] + _mod_rows(m_ref, 2, b, is_ctx) * y
    h = _rms(x1, g2_ref[...])
    h = (h * (1.0 + _mod_rows(m_ref, 4, b, is_ctx)) + _mod_rows(m_ref, 3, b, is_ctx)).astype(BF16)
    acc = None
    for f0 in range(0, D_FF, tf):
        g = jnp.dot(h, wg_ref[:, f0:f0 + tf], preferred_element_type=F32)
        u = jnp.dot(h, wu_ref[:, f0:f0 + tf], preferred_element_type=F32)
        a = (g * jax.nn.sigmoid(g) * u).astype(BF16)
        t = jnp.dot(a, wd_ref[f0:f0 + tf, :], preferred_element_type=F32)
        acc = t if acc is None else acc + t
    o_ref[...] = x1 + _mod_rows(m_ref, 5, b, is_ctx) * acc


def post_mixer(x_all, m, g2, mixes, wg, wu, wd, n_lat, n_rows, bm, tf=1408):
    B, T, D = x_all.shape
    row = lambda n: pl.BlockSpec((None, bm, n), lambda b, i: (b, i, 0))
    mix_args, mix_specs = [], []
    for a, wmat in mixes:
        mix_args += [a, wmat]
        mix_specs += [row(a.shape[-1]), _const_spec(wmat.shape, True)]
    return pl.pallas_call(
        functools.partial(_post_kernel, len(mixes), n_lat, tf),
        grid=(B, n_rows // bm),
        in_specs=[row(D), _const_spec(m.shape), _const_spec(g2.shape)] + mix_specs
        + [_const_spec(wg.shape, True), _const_spec(wu.shape, True), _const_spec(wd.shape, True)],
        out_specs=row(D),
        out_shape=jax.ShapeDtypeStruct((B, n_rows, D), F32),
        compiler_params=_cparams("parallel", "parallel"),
        name="post_mixer",
    )(x_all, m, g2, *mix_args, wg, wu, wd)


def _odd_pre_kernel(n_lat, x_ref, m_ref, g1_ref, w_ref, gate_ref, xr_ref):
    b = pl.program_id(0)
    i = pl.program_id(1)
    bm = x_ref.shape[0]
    is_ctx = _ctx_rows(i, bm, n_lat)
    h = _rms(x_ref[...], g1_ref[...])
    h = h * (1.0 + _mod_rows(m_ref, 1, b, is_ctx)) + _mod_rows(m_ref, 0, b, is_ctx)
    u = jnp.dot(h.astype(BF16), w_ref[...], preferred_element_type=F32)
    gate_ref[...] = u[:, :D_RNN].astype(BF16)
    xr_ref[...] = u[:, D_RNN:]


def odd_pre(x_all, m, g1, w_in, n_lat, bm):
    B, T, D = x_all.shape
    row = lambda n: pl.BlockSpec((None, bm, n), lambda b, i: (b, i, 0))
    return pl.pallas_call(
        functools.partial(_odd_pre_kernel, n_lat),
        grid=(B, T // bm),
        in_specs=[row(D), _const_spec(m.shape), _const_spec(g1.shape), _const_spec(w_in.shape)],
        out_specs=[row(D_RNN), row(D_RNN)],
        out_shape=[jax.ShapeDtypeStruct((B, T, D_RNN), BF16), jax.ShapeDtypeStruct((B, T, D_RNN), F32)],
        compiler_params=_cparams("parallel", "parallel"),
        name="odd_pre",
    )(x_all, m, g1, w_in)


def _rglru_kernel(n_lat, chunk, gate_ref, xr_ref, cw_ref, cb_ref, wg_ref, bg_ref, lam_ref, o_ref, hf_ref):
    T = xr_ref.shape[0]
    n_ctx = T - n_lat
    R = chunk
    nv = R // 8
    cw = cw_ref[...]
    cb = cb_ref[...]
    sub = lax.broadcasted_iota(jnp.int32, (nv, 8, RG_BW), 1)

    def conv_chunk(c0, seg_lo, seg_hi):
        lo = jnp.maximum(c0 - 8, 0)
        hi = jnp.minimum(c0 + R, T - 8)
        win = jnp.concatenate([xr_ref[pl.ds(pl.multiple_of(lo, 8), 8), :],
                               xr_ref[pl.ds(pl.multiple_of(c0, 8), R), :],
                               xr_ref[pl.ds(pl.multiple_of(hi, 8), 8), :]], axis=0)
        g = c0 - 8 + lax.broadcasted_iota(jnp.int32, (R + 16, 1), 0)
        win = jnp.where((g >= seg_lo) & (g < seg_hi), win, 0.0)
        y = cb
        for kk in range(RG_CONV):
            off = 8 + kk - RG_CONV // 2
            y = y + win[off:off + R, :] * cw[kk:kk + 1, :]
        return y

    def gates(xc, d):
        pre = jnp.dot(xc.astype(BF16), wg_ref[d], preferred_element_type=F32) + bg_ref[d]
        r = jax.nn.sigmoid(pre[:, :RG_BW])
        ig = jax.nn.sigmoid(pre[:, RG_BW:])
        log_a = -RG_C * r * jax.nn.softplus(-lam_ref[d])
        a = jnp.exp(log_a)
        bt = jnp.sqrt(-jnp.tanh(log_a) * (a * a + 1.0)) * (ig * xc)
        return a, bt

    def local_scan(a, bt, reverse):
        a = a.reshape(nv, 8, RG_BW)
        bt = bt.reshape(nv, 8, RG_BW)
        for d in (1, 2, 4):
            if reverse:
                keep = sub < 8 - d
                sh = 8 - d
            else:
                keep = sub >= d
                sh = d
            a_s = jnp.where(keep, pltpu.roll(a, sh, 1), 1.0)
            b_s = jnp.where(keep, pltpu.roll(bt, sh, 1), 0.0)
            bt = a * b_s + bt
            a = a * a_s
        return a, bt

    def chunk_scan(a, bt, h, reverse):
        a, bt = local_scan(a, bt, reverse)
        rows = [None] * nv
        order = range(nv - 1, -1, -1) if reverse else range(nv)
        edge = 0 if reverse else 7
        for v in order:
            hv = bt[v] + a[v] * h
            rows[v] = hv
            h = hv[edge:edge + 1, :]
        return jnp.concatenate(rows, axis=0), h

    n_lat_chunks = n_lat // R
    h0 = jnp.zeros((1, RG_BW), F32)

    def fwd_chunk(c0, seg_lo, seg_hi, h):
        xc = conv_chunk(c0, seg_lo, seg_hi)
        a, bt = gates(xc, 0)
        hs, h = chunk_scan(a, bt, h, False)
        hf_ref[pl.ds(pl.multiple_of(c0, 8), R), :] = hs
        return h

    h = h0
    for c in range(n_ctx // R):
        h = fwd_chunk(n_lat + c * R, n_lat, T, h)
    h = lax.fori_loop(0, n_lat_chunks, lambda c, hh: fwd_chunk(c * R, 0, n_lat, hh), h)

    def bwd_chunk(c0, seg_lo, seg_hi, h):
        xc = conv_chunk(c0, seg_lo, seg_hi)
        a, bt = gates(xc, 1)
        hs, h = chunk_scan(a, bt, h, True)
        rs = pl.ds(pl.multiple_of(c0, 8), R)
        g = gate_ref[rs, :].astype(F32)
        o_ref[rs, :] = (jax.nn.gelu(g) * (hf_ref[rs, :] + hs)).astype(o_ref.dtype)
        return h

    h = h0
    for c in range(n_ctx // R - 1, -1, -1):
        h = bwd_chunk(n_lat + c * R, n_lat, T, h)
    lax.fori_loop(0, n_lat_chunks, lambda c, hh: bwd_chunk((n_lat_chunks - 1 - c) * R, 0, n_lat, hh), h)


def rglru(gate, xr, cw, cb, wg, bg, lam, n_lat, chunk=256):
    B, T, _ = xr.shape
    col = pl.BlockSpec((None, T, RG_BW), lambda b, n: (b, 0, n))
    return pl.pallas_call(
        functools.partial(_rglru_kernel, n_lat, chunk),
        grid=(B, RG_BLOCKS),
        in_specs=[
            col, col,
            pl.BlockSpec((RG_CONV, RG_BW), lambda b, n: (0, n)),
            pl.BlockSpec((1, RG_BW), lambda b, n: (0, n)),
            pl.BlockSpec((None, 2, RG_BW, 2 * RG_BW), lambda b, n: (n, 0, 0, 0)),
            pl.BlockSpec((None, 2, 1, 2 * RG_BW), lambda b, n: (n, 0, 0, 0)),
            pl.BlockSpec((None, 2, 1, RG_BW), lambda b, n: (n, 0, 0, 0)),
        ],
        out_specs=col,
        out_shape=jax.ShapeDtypeStruct((B, T, D_RNN), BF16),
        scratch_shapes=[pltpu.VMEM((T, RG_BW), F32)],
        compiler_params=_cparams("parallel", "parallel"),
        name="rglru",
    )(gate, xr, cw, cb, wg, bg, lam)


def _hyena_filters(L, w1, b1, w2, b2, w3, freq):
    k = jnp.arange(L, dtype=F32)
    t = (k / max(L - 1, 1))[:, None]
    ang = (2.0 * math.pi / L) * k[:, None] * jnp.linspace(1e-4, HY_BANDS - 1, HY_BANDS, dtype=F32)[None, :]
    z = jnp.concatenate([t, jnp.cos(ang), -jnp.sin(ang)], axis=-1)
    mm = functools.partial(jnp.dot, precision=lax.Precision.HIGHEST)
    h = jnp.sin(freq * (mm(z, w1) + b1))
    h = jnp.sin(freq * (mm(h, w2) + b2))
    h = mm(h, w3).reshape(L, 2, D_HY)
    half = L // 2
    dist = jnp.abs(k - half) / max(half, 1)
    deltas = jnp.abs(jnp.linspace(math.log(HY_TARGET) / HY_SLOW_DECAY, math.log(HY_TARGET) / HY_FAST_DECAY,
                                  D_HY, dtype=F32))
    window = jnp.exp(-dist[:, None] * deltas[None, :])
    return jnp.transpose(h * window[:, None, :], (1, 0, 2))


def _swap_neg(x):
    return jnp.concatenate([x[:, HY_CB:], -x[:, :HY_CB]], axis=1)


def _cstack(x):
    return jnp.concatenate([x, _swap_neg(x)], axis=0)


def _cmul(h, w):
    hre = jnp.concatenate([h[:, :HY_CB], h[:, :HY_CB]], axis=1)
    him = jnp.concatenate([-h[:, HY_CB:], h[:, HY_CB:]], axis=1)
    wsw = jnp.concatenate([w[:, HY_CB:], w[:, :HY_CB]], axis=1)
    return hre * w + him * wsw


def _hy_short_kernel(u_ref, w_ref, b_ref, o_ref):
    L = u_ref.shape[1]
    row = lax.broadcasted_iota(jnp.int32, (L, 1), 0)
    w = w_ref[...]
    halves = []
    for b in range(2):
        u = u_ref[b].astype(F32)
        prev = jnp.where(row >= 1, pltpu.roll(u, 1, 0), 0.0)
        nxt = jnp.where(row < L - 1, pltpu.roll(u, L - 1, 0), 0.0)
        halves.append(prev * w[0:1] + u * w[1:2] + nxt * w[2:3] + b_ref[...])
    o_ref[...] = jnp.concatenate(halves, axis=1).astype(o_ref.dtype)


def hy_short(u_hy, short_w, short_b, row_blk, L):
    return pl.pallas_call(
        _hy_short_kernel,
        grid=(3, HY_NCB),
        in_specs=[
            pl.BlockSpec((2, L, HY_CB), lambda g, cb: (0, row_blk, g * HY_NCB + cb)),
            pl.BlockSpec((3, HY_CB), lambda g, cb: (0, g * HY_NCB + cb)),
            pl.BlockSpec((1, HY_CB), lambda g, cb: (0, g * HY_NCB + cb)),
        ],
        out_specs=pl.BlockSpec((None, None, L, 2 * HY_CB), lambda g, cb: (g, cb, 0, 0)),
        out_shape=jax.ShapeDtypeStruct((3, HY_NCB, L, 2 * HY_CB), BF16),
        compiler_params=_cparams("parallel", "parallel"),
        name="hy_short",
    )(u_hy, short_w, short_b)


def _hy_dft_kernel(x_ref, m_ref, o_ref):
    for j in range(x_ref.shape[0]):
        o_ref[j] = jnp.dot(m_ref[j], _cstack(x_ref[j]), preferred_element_type=F32).astype(o_ref.dtype)


def hy_dft_per_slab(x, mats, n_out, chunk=16):
    ncb, ns, rows, lanes = x.shape
    return pl.pallas_call(
        _hy_dft_kernel,
        grid=(ncb, ns // chunk),
        in_specs=[
            pl.BlockSpec((None, chunk, rows, lanes), lambda cb, s: (cb, s, 0, 0)),
            pl.BlockSpec((chunk, n_out, 2 * rows), lambda cb, s: (s, 0, 0)),
        ],
        out_specs=pl.BlockSpec((None, chunk, n_out, lanes), lambda cb, s: (cb, s, 0, 0)),
        out_shape=jax.ShapeDtypeStruct((ncb, ns, n_out, lanes), BF16),
        compiler_params=_cparams("parallel", "parallel"),
        name="hy_dft_slab",
    )(x, mats)


def _hy_mid_kernel(with_filter, a_ref, fb_ref, *rest):
    if with_filter:
        h_ref, fc_ref, o_ref = rest
    else:
        (o_ref,) = rest
    for k in range(a_ref.shape[0]):
        w = jnp.dot(fb_ref[...], _cstack(a_ref[k]), preferred_element_type=F32)
        if with_filter:
            y = _cmul(h_ref[k].astype(F32), w)
            w = jnp.dot(fc_ref[...], _cstack(y.astype(BF16)), preferred_element_type=F32)
        o_ref[k] = w.astype(o_ref.dtype)


def hy_mid(a, fb, spec=None, fc=None, chunk=16):
    ncb, ns, rows, lanes = a.shape
    slab = pl.BlockSpec((None, chunk, rows, lanes), lambda cb, s: (cb, s, 0, 0))
    args, specs = [a, fb], [slab, _const_spec(fb.shape)]
    if spec is not None:
        args += [spec, fc]
        specs += [slab, _const_spec(fc.shape)]
    return pl.pallas_call(
        functools.partial(_hy_mid_kernel, spec is not None),
        grid=(ncb, ns // chunk),
        in_specs=specs,
        out_specs=slab,
        out_shape=jax.ShapeDtypeStruct(a.shape, BF16),
        compiler_params=_cparams("parallel", "parallel"),
        name="hy_mid",
    )(*args)


def _hy_gate_kernel(unpack, x_ref, y_ref, z_ref, b_ref, o_ref):
    z = x_ref[...].astype(F32) * (y_ref[...].astype(F32) + b_ref[...] * z_ref[...].astype(F32))
    if unpack:
        o_ref[0] = z[:, :HY_CB].astype(o_ref.dtype)
        o_ref[1] = z[:, HY_CB:].astype(o_ref.dtype)
    else:
        o_ref[...] = z.astype(o_ref.dtype)


def hy_gate(x, y, z, bias_p, unpack, rows=2048):
    ncb, L, lanes = x.shape
    blk = pl.BlockSpec((None, rows, lanes), lambda cb, r: (cb, r, 0))
    if unpack:
        out_spec = pl.BlockSpec((2, rows, HY_CB), lambda cb, r: (0, r, cb))
        out_shape = jax.ShapeDtypeStruct((2, L, ncb * HY_CB), BF16)
    else:
        out_spec, out_shape = blk, jax.ShapeDtypeStruct(x.shape, BF16)
    return pl.pallas_call(
        functools.partial(_hy_gate_kernel, unpack),
        grid=(ncb, L // rows),
        in_specs=[blk, blk, blk, pl.BlockSpec((None, 1, lanes), lambda cb, r: (cb, 0, 0))],
        out_specs=out_spec,
        out_shape=out_shape,
        compiler_params=_cparams("parallel", "parallel"),
        name="hy_gate",
    )(x, y, z, bias_p)


def _hy_ctx_kernel(s_ref, h_ref, b_ref, ff_ref, fi_ref, o_ref):
    z = s_ref[0].astype(F32)
    L = z.shape[0]
    ff = ff_ref[...]
    for o in range(2):
        hp = jnp.concatenate([h_ref[o], jnp.zeros((L, HY_CB), F32)], axis=1)
        spec = jnp.dot(ff, _cstack(hp.astype(BF16)), preferred_element_type=F32)
        w = jnp.dot(ff, _cstack(z.astype(BF16)), preferred_element_type=F32)
        y = jnp.dot(fi_ref[...], _cstack(_cmul(spec, w).astype(BF16)), preferred_element_type=F32)
        z = s_ref[o + 1].astype(F32) * (y + b_ref[o] * z)
    o_ref[0] = z[:, :HY_CB].astype(o_ref.dtype)
    o_ref[1] = z[:, HY_CB:].astype(o_ref.dtype)


def hy_ctx(s, h, bias_p, ff, fi):
    _, ncb, L, lanes = s.shape
    return pl.pallas_call(
        _hy_ctx_kernel,
        grid=(ncb,),
        in_specs=[
            pl.BlockSpec((3, None, L, lanes), lambda cb: (0, cb, 0, 0)),
            pl.BlockSpec((2, L, HY_CB), lambda cb: (0, 0, cb)),
            pl.BlockSpec((2, None, 1, lanes), lambda cb: (0, cb, 0, 0)),
            _const_spec(ff.shape), _const_spec(fi.shape),
        ],
        out_specs=pl.BlockSpec((2, L, HY_CB), lambda cb: (0, 0, cb)),
        out_shape=jax.ShapeDtypeStruct((2, L, ncb * HY_CB), BF16),
        compiler_params=_cparams("parallel"),
        name="hy_ctx",
    )(s, h, bias_p, ff, fi)


def _hy_tables(L, L_ctx):
    n = 2 * L
    n1 = L // HY_N
    two_pi = 2.0 * math.pi
    ar = lambda m: jnp.arange(m, dtype=jnp.int32)
    cs = lambda m, period: (jnp.cos(m.astype(F32) * (two_pi / period)), jnp.sin(m.astype(F32) * (two_pi / period)))
    m = (ar(HY_N)[None, :, None] * (HY_N * ar(n1)[None, None, :] + ar(HY_N)[:, None, None])) % n
    c, s = cs(m, n)
    ma = jnp.concatenate([c, s], axis=-1).astype(BF16)
    q1 = ar(n1) + n1 // 2
    m = (ar(HY_N)[None, None, :] * (HY_N * q1[None, :, None] + ar(HY_N)[:, None, None])) % n
    c, s = cs(m, n)
    md = (jnp.concatenate([c, -s], axis=-1) / n).astype(BF16)
    m = (ar(HY_N)[:, None] * ar(HY_N)[None, :]) % HY_N
    c, s = cs(m, HY_N)
    fb = jnp.concatenate([c, s], axis=-1).astype(BF16)
    fc = jnp.concatenate([c, -s], axis=-1).astype(BF16)
    nc = 2 * L_ctx
    m = (ar(nc)[:, None] * ar(L_ctx)[None, :]) % nc
    c, s = cs(m, nc)
    ff = jnp.concatenate([c, s], axis=-1).astype(BF16)
    m = ((ar(L_ctx) + L_ctx // 2)[:, None] * ar(nc)[None, :]) % nc
    c, s = cs(m, nc)
    fi = (jnp.concatenate([c, -s], axis=-1) / nc).astype(BF16)
    return {"ma": ma, "md": md, "fb": fb, "fc": fc, "ff": ff, "fi": fi}


def _slabs(x, n_major):
    ncb, L, lanes = x.shape
    return jnp.swapaxes(x.reshape(ncb, n_major, L // n_major, lanes), 1, 2)


def hyena(u_hy, short_w, short_b, filt, bias, tabs, n_lat, n_ctx):
    n1 = n_lat // HY_N
    bias_p = jnp.concatenate([bias.reshape(2, HY_NCB, 1, HY_CB)] * 2, axis=-1)
    h_lat = _hyena_filters(n_lat, *filt)
    spectra = []
    for o in range(2):
        hp = jnp.transpose(h_lat[o].reshape(n_lat, HY_NCB, HY_CB), (1, 0, 2))
        hp = jnp.concatenate([hp, jnp.zeros_like(hp)], axis=-1).astype(BF16)
        a = hy_dft_per_slab(_slabs(hp, n1), tabs["ma"], HY_N)
        spectra.append(hy_mid(jnp.swapaxes(a, 1, 2), tabs["fb"]))
    s = hy_short(u_hy, short_w, short_b, 0, n_lat)
    z = s[0]
    for o in range(2):
        a = hy_dft_per_slab(_slabs(z, n1), tabs["ma"], HY_N)
        v = hy_mid(jnp.swapaxes(a, 1, 2), tabs["fb"], spectra[o], tabs["fc"])
        y = hy_dft_per_slab(jnp.swapaxes(v, 1, 2), tabs["md"], n1)
        y = jnp.swapaxes(y, 1, 2).reshape(HY_NCB, n_lat, 2 * HY_CB)
        z = hy_gate(s[o + 1], y, z, bias_p[o], o == 1)
    s_c = hy_short(u_hy, short_w, short_b, n_lat // n_ctx, n_ctx)
    z_c = hy_ctx(s_c, _hyena_filters(n_ctx, *filt), bias_p, tabs["ff"], tabs["fi"])
    return jnp.concatenate([z, z_c], axis=1)


def _rope_tables(n_lat, n_ctx):
    rows = n_lat // GRID_W
    row = jnp.repeat(jnp.arange(rows, dtype=F32), GRID_W)
    col = jnp.tile(jnp.arange(GRID_W, dtype=F32), rows)
    inv_freq = ROPE_BASE ** (-jnp.arange(0, ROPE_AXIS, 2, dtype=F32) / ROPE_AXIS)
    ang = jnp.concatenate([row[:, None] * inv_freq[None, :], col[:, None] * inv_freq[None, :]], axis=-1)
    ang = jnp.concatenate([ang, jnp.zeros((n_ctx, ROPE_AXIS), F32)], axis=0)
    cos, sin = jnp.cos(ang), jnp.sin(ang)
    T = n_lat + n_ctx
    z8 = jnp.zeros((T, ROPE_AXIS // 2), F32)
    cos32 = jnp.concatenate([cos[:, :8], cos[:, :8], cos[:, 8:], cos[:, 8:]], axis=-1)
    first = jnp.concatenate([-sin[:, :8], z8, -sin[:, 8:], z8], axis=-1)
    second = jnp.concatenate([z8, sin[:, :8], z8, sin[:, 8:]], axis=-1)
    pad = lambda t, fill: jnp.concatenate(
        [jnp.full((T, MLA_NOPE), fill, F32), t, jnp.zeros((T, HEAD_LANES - MLA_NOPE - MLA_ROPE), F32)], axis=-1)
    lane = jnp.arange(HEAD_LANES)
    seg = jnp.where(lane < MLA_NOPE, 0, jnp.where(lane < MLA_NOPE + MLA_ROPE, 1, 2))
    msq = (seg[:, None] == seg[None, :]) & (seg[:, None] < 2)
    msq = jnp.where(msq, jnp.where(seg[:, None] == 0, 1.0 / MLA_NOPE, 1.0 / MLA_ROPE), 0.0).astype(BF16)
    return {"ct": pad(cos32, 1.0), "s1": pad(first, 0.0), "s2": pad(second, 0.0), "msq": msq}


def _head_gain(g, scale):
    return jnp.concatenate([g * scale, jnp.zeros((HEAD_LANES - g.shape[0],), F32)])[None, :]


def _even_weights(j, ev_w_in, mla_q_norm_g, mla_w_uq, mla_kv_norm_g, mla_w_ukv, mla_qk_g_q, mla_qk_g_k):
    i_q = 3 * D_HY
    i_kv = i_q + MLA_Q_RANK
    i_kr = i_kv + MLA_KV_RANK
    w_in = ev_w_in[j]
    d = w_in.shape[0]
    kr_group = jnp.concatenate([jnp.zeros((d, MLA_NOPE), F32), w_in[:, i_kr:],
                                jnp.zeros((d, HEAD_LANES - MLA_NOPE - MLA_ROPE), F32)], axis=-1)
    w_in_p = jnp.concatenate([w_in[:, :i_kr], kr_group], axis=-1).astype(BF16)
    hq = MLA_NOPE + MLA_ROPE
    w_uq = mla_w_uq[j].reshape(MLA_Q_RANK, MLA_HEADS, hq)
    w_uq = jnp.pad(w_uq, ((0, 0), (0, 0), (0, HEAD_LANES - hq))).reshape(MLA_Q_RANK, -1).astype(BF16)
    w_ukv = mla_w_ukv[j].reshape(MLA_KV_RANK, MLA_HEADS, MLA_NOPE + MLA_V)
    w_uk = jnp.pad(w_ukv[:, :, :MLA_NOPE], ((0, 0), (0, 0), (0, HEAD_LANES - MLA_NOPE)))
    w_uk = w_uk.reshape(MLA_KV_RANK, -1).astype(BF16)
    w_uv = w_ukv[:, :, MLA_NOPE:].reshape(MLA_KV_RANK, -1).astype(BF16)
    return {
        "w_in": w_in_p, "q_norm_g": mla_q_norm_g[j][None, :], "kv_norm_g": mla_kv_norm_g[j][None, :],
        "w_uq": w_uq, "w_uk": w_uk, "w_uv": w_uv,
        "gq": _head_gain(mla_qk_g_q[j], MLA_SCALE), "gk": _head_gain(mla_qk_g_k[j], 1.0),
    }


def _rg_gate_weights(j, rg_w_a, rg_b_a, rg_w_x, rg_b_x, rg_lambda):
    wg = jnp.concatenate([rg_w_a[j], rg_w_x[j]], axis=-1)
    wg = jnp.transpose(wg, (1, 0, 2, 3)).astype(BF16)
    blk = lambda t: jnp.transpose(t.reshape(2, RG_BLOCKS, RG_BW), (1, 0, 2))
    bg = jnp.concatenate([blk(rg_b_a[j]), blk(rg_b_x[j])], axis=-1)[:, :, None, :]
    lam = blk(rg_lambda[j])[:, :, None, :]
    return wg, bg, lam


def kernel(x, c, ctx, c_ctx, ada_w, ada_b, norm1_g, norm2_g, ffn_w_in, ffn_w_out, ev_w_in, hy_short_w, hy_short_b, hy_w1, hy_b1, hy_w2, hy_b2, hy_w3, hy_freq, hy_bias, mla_q_norm_g, mla_w_uq, mla_kv_norm_g, mla_w_ukv, mla_qk_g_q, mla_qk_g_k, ev_w_out, rg_w_in, rg_conv_w, rg_conv_b, rg_w_a, rg_b_a, rg_w_x, rg_b_x, rg_lambda, rg_w_out):
    B, n_lat, D = x.shape
    n_ctx = ctx.shape[1]
    T = n_lat + n_ctx
    bm = 768
    assert T % bm == 0 and n_lat % 512 == 0 and n_ctx == 256

    cond8 = jnp.concatenate([c, c_ctx[None, :], jnp.zeros((8 - B - 1, D), F32)], axis=0)
    mods = ada_modulation(cond8, ada_w, ada_b)
    tabs = _rope_tables(n_lat, n_ctx)
    hy_tabs = _hy_tables(n_lat, n_ctx)
    x_all = jnp.concatenate([x, ctx], axis=1)

    for i in range(DEPTH):
        j = i // 2
        last = i == DEPTH - 1
        m = mods[i]
        g1 = norm1_g[i][None, :]
        g2 = norm2_g[i][None, :]
        if i % 2 == 0:
            w = _even_weights(j, ev_w_in, mla_q_norm_g, mla_w_uq, mla_kv_norm_g, mla_w_ukv, mla_qk_g_q, mla_qk_g_k)
            u_hy, q, k, v = even_pre(x_all, m, g1, w, tabs, n_lat, bm)
            att_l = attention(q, k, v, 0, n_lat, 512, 0, T, 768)
            att_c = attention(q, k, v, n_lat // n_ctx, n_ctx, n_ctx, n_lat // n_ctx, n_ctx, n_ctx)
            att = jnp.concatenate([att_l, att_c], axis=1)
            filt = (hy_w1[j], hy_b1[j], hy_w2[j], hy_b2[j], hy_w3[j], hy_freq[j])
            hy = hyena(u_hy, hy_short_w[j], hy_short_b[j][None, :], filt, hy_bias[j], hy_tabs, n_lat, n_ctx)
            w_out = ev_w_out[j].astype(BF16)
            mixes = [(hy, w_out[:D_HY]), (att, w_out[D_HY:])]
        else:
            gate, xr = odd_pre(x_all, m, g1, rg_w_in[j].astype(BF16), n_lat, bm)
            wg, bg, lam = _rg_gate_weights(j, rg_w_a, rg_b_a, rg_w_x, rg_b_x, rg_lambda)
            y = rglru(gate, xr, rg_conv_w[j], rg_conv_b[j][None, :], wg, bg, lam, n_lat)
            mixes = [(y, rg_w_out[j].astype(BF16))]
        wg_f = ffn_w_in[i][:, :D_FF].astype(BF16)
        wu_f = ffn_w_in[i][:, D_FF:].astype(BF16)
        wd_f = ffn_w_out[i].astype(BF16)
        if last:
            x_all = post_mixer(x_all, m, g2, mixes, wg_f, wu_f, wd_f, n_lat, n_lat, 512)
        else:
            x_all = post_mixer(x_all, m, g2, mixes, wg_f, wu_f, wd_f, n_lat, T, bm)
    return x_all
```

```python
import functools
import math

import jax
import jax.numpy as jnp
from jax import lax
from jax.experimental import pallas as pl
from jax.experimental.pallas import tpu as pltpu

F32 = jnp.float32
BF16 = jnp.bfloat16

D_MODEL = 1024
DEPTH = 4
GRID_W = 64
D_FF = 2816
NORM_EPS = 1e-6

D_HY = 512
HY_CB = 128
HY_NCB = D_HY // HY_CB
HY_N = 128
HY_BANDS = 16
HY_FAST_DECAY = 0.3
HY_SLOW_DECAY = 1.5
HY_TARGET = 1e-2

MLA_HEADS = 8
MLA_NOPE = 64
MLA_ROPE = 32
MLA_V = 64
MLA_Q_RANK = 384
MLA_KV_RANK = 256
MLA_SCALE = (MLA_NOPE + MLA_ROPE) ** -0.5
ROPE_AXIS = 16
ROPE_BASE = 10000.0
HEAD_LANES = 128

D_RNN = 1280
RG_BLOCKS = 10
RG_BW = 128
RG_CONV = 4
RG_C = 8.0

EVEN_IN_PAD = 3 * D_HY + MLA_Q_RANK + MLA_KV_RANK + HEAD_LANES

VMEM_LIMIT = 56 * 1024 * 1024


def _cparams(*sem):
    return pltpu.CompilerParams(dimension_semantics=sem, vmem_limit_bytes=VMEM_LIMIT)


def _const_spec(shape, single_buffer=False):
    nd = len(shape)
    kw = {"pipeline_mode": pl.Buffered(1)} if single_buffer else {}
    return pl.BlockSpec(shape, lambda *_: (0,) * nd, **kw)


def _ada_kernel(s_ref, w_ref, b_ref, o_ref):
    s = s_ref[...]
    s = s * jax.nn.sigmoid(s)
    o_ref[...] = jnp.dot(s, w_ref[...], preferred_element_type=F32) + b_ref[...]


def ada_modulation(cond8, ada_w, ada_b):
    depth, d, n = ada_w.shape
    tn = 1536
    return pl.pallas_call(
        _ada_kernel,
        grid=(depth, n // tn),
        in_specs=[
            pl.BlockSpec((8, d), lambda i, j: (0, 0)),
            pl.BlockSpec((None, d, tn), lambda i, j: (i, 0, j)),
            pl.BlockSpec((None, 1, tn), lambda i, j: (i, 0, j)),
        ],
        out_specs=pl.BlockSpec((None, 8, tn), lambda i, j: (i, 0, j)),
        out_shape=jax.ShapeDtypeStruct((depth, 8, n), F32),
        compiler_params=_cparams("parallel", "parallel"),
        name="ada_modulation",
    )(cond8, ada_w, ada_b.reshape(depth, 1, n))


def _mod_rows(m_ref, idx, b, is_ctx):
    d = D_MODEL
    lat = m_ref[pl.ds(b, 1), pl.ds(idx * d, d)]
    ctx = m_ref[pl.ds(2, 1), pl.ds(idx * d, d)]
    return jnp.where(is_ctx, ctx, lat)


def _rms(x, g):
    ms = jnp.mean(x * x, axis=-1, keepdims=True)
    return x * lax.rsqrt(ms + NORM_EPS) * g


def _ctx_rows(i, bm, n_lat):
    row = i * bm + lax.broadcasted_iota(jnp.int32, (bm, 1), 0)
    return row >= n_lat


def _even_pre_kernel(n_lat, x_ref, m_ref, g1_ref, w_in_ref, qg_ref, kvg_ref, w_uq_ref, w_uk_ref,
                     w_uv_ref, gq_ref, gk_ref, msq_ref, ct_ref, s1_ref, s2_ref,
                     hy_ref, q_ref, k_ref, v_ref):
    b = pl.program_id(0)
    i = pl.program_id(1)
    bm = x_ref.shape[0]
    is_ctx = _ctx_rows(i, bm, n_lat)
    x = x_ref[...]
    h = _rms(x, g1_ref[...])
    h = h * (1.0 + _mod_rows(m_ref, 1, b, is_ctx)) + _mod_rows(m_ref, 0, b, is_ctx)
    u = jnp.dot(h.astype(BF16), w_in_ref[...], preferred_element_type=F32)
    i_q = 3 * D_HY
    i_kv = i_q + MLA_Q_RANK
    i_kr = i_kv + MLA_KV_RANK
    hy_ref[...] = u[:, :i_q].astype(BF16)

    ct = ct_ref[...]
    s1 = s1_ref[...]
    s2 = s2_ref[...]
    msq = msq_ref[...]

    def head_norm(t, g):
        ms = jnp.dot((t * t).astype(BF16), msq, preferred_element_type=F32)
        return t * lax.rsqrt(ms + NORM_EPS) * g

    def rope(t):
        return (t * ct + pltpu.roll(t, HEAD_LANES - ROPE_AXIS // 2, 1) * s1
                + pltpu.roll(t, ROPE_AXIS // 2, 1) * s2)

    qn = _rms(u[:, i_q:i_kv], qg_ref[...]).astype(BF16)
    kvn = _rms(u[:, i_kv:i_kr], kvg_ref[...]).astype(BF16)
    gq = gq_ref[...]
    gk = gk_ref[...]
    k_rope = rope(head_norm(u[:, i_kr:], gk))
    for hd in range(MLA_HEADS):
        sl = slice(hd * HEAD_LANES, (hd + 1) * HEAD_LANES)
        qh = jnp.dot(qn, w_uq_ref[:, sl], preferred_element_type=F32)
        q_ref[hd] = rope(head_norm(qh, gq)).astype(BF16)
        kh = jnp.dot(kvn, w_uk_ref[:, sl], preferred_element_type=F32)
        k_ref[hd] = (head_norm(kh, gk) + k_rope).astype(BF16)
    v = jnp.dot(kvn, w_uv_ref[...], preferred_element_type=F32)
    for p in range(MLA_HEADS // 2):
        v_ref[p] = v[:, p * HEAD_LANES:(p + 1) * HEAD_LANES].astype(BF16)


def even_pre(x_all, m, g1, w, tabs, n_lat, bm):
    B, T, D = x_all.shape
    H = MLA_HEADS
    row = lambda n: pl.BlockSpec((None, bm, n), lambda b, i: (b, i, 0))
    tab = pl.BlockSpec((bm, HEAD_LANES), lambda b, i: (i, 0))
    headed = lambda nh: pl.BlockSpec((None, nh, bm, HEAD_LANES), lambda b, i: (b, 0, i, 0))
    consts = [m, g1, w["w_in"], w["q_norm_g"], w["kv_norm_g"], w["w_uq"], w["w_uk"], w["w_uv"],
              w["gq"], w["gk"], tabs["msq"]]
    return pl.pallas_call(
        functools.partial(_even_pre_kernel, n_lat),
        grid=(B, T // bm),
        in_specs=[row(D)] + [_const_spec(c.shape) for c in consts] + [tab, tab, tab],
        out_specs=[row(3 * D_HY), headed(H), headed(H), headed(H // 2)],
        out_shape=[
            jax.ShapeDtypeStruct((B, T, 3 * D_HY), BF16),
            jax.ShapeDtypeStruct((B, H, T, HEAD_LANES), BF16),
            jax.ShapeDtypeStruct((B, H, T, HEAD_LANES), BF16),
            jax.ShapeDtypeStruct((B, H // 2, T, HEAD_LANES), BF16),
        ],
        compiler_params=_cparams("parallel", "parallel"),
        name="even_pre",
    )(x_all, *consts, tabs["ct"], tabs["s1"], tabs["s2"])


def _attn_kernel(k_ref, qt_ref, vt_ref, o_ref, s_ref):
    n_chunks = k_ref.shape[1]
    tq = qt_ref.shape[2]
    assert n_chunks % 2 == 1

    def scores(c, slot):
        for hh in range(2):
            s_ref[slot, hh] = jnp.dot(k_ref[hh, c], qt_ref[hh], preferred_element_type=F32)

    def absorb(c, slot, stats):
        out = []
        for hh in range(2):
            m, l, acc = stats[hh]
            m_new = jnp.maximum(m, jnp.max(s_ref[slot, hh], axis=0, keepdims=True))
            alpha = jnp.exp2(m - m_new)
            p = jnp.exp2(s_ref[slot, hh] - m_new)
            l = alpha * l + jnp.sum(p, axis=0, keepdims=True)
            acc = alpha * acc + jnp.dot(vt_ref[hh, c], p.astype(BF16), preferred_element_type=F32)
            out.append((m_new, l, acc))
        return tuple(out)

    def body(i, stats):
        c = 2 * i
        scores(c + 1, 1)
        stats = absorb(c, 0, stats)
        scores(c + 2, 0)
        return absorb(c + 1, 1, stats)

    init = tuple((jnp.full((1, tq), -jnp.inf, F32), jnp.zeros((1, tq), F32), jnp.zeros((MLA_V, tq), F32))
                 for _ in range(2))
    scores(0, 0)
    stats = lax.fori_loop(0, n_chunks // 2, body, init)
    res = absorb(n_chunks - 1, 0, stats)
    o_ref[...] = jnp.concatenate([acc / l for _, l, acc in res], axis=0).astype(o_ref.dtype)


def attention(k_chunks, q_t, v_t, tq):
    B, H, n_chunks, tk, _ = k_chunks.shape
    n_q = q_t.shape[-1]
    return pl.pallas_call(
        _attn_kernel,
        grid=(B, H // 2, n_q // tq),
        in_specs=[
            pl.BlockSpec((None, 2, n_chunks, tk, HEAD_LANES), lambda b, h, i: (b, h, 0, 0, 0)),
            pl.BlockSpec((None, 2, HEAD_LANES, tq), lambda b, h, i: (b, h, 0, i)),
            pl.BlockSpec((None, 2, n_chunks, MLA_V, tk), lambda b, h, i: (b, h, 0, 0, 0)),
        ],
        out_specs=pl.BlockSpec((None, None, 2 * MLA_V, tq), lambda b, h, i: (b, h, 0, i)),
        out_shape=jax.ShapeDtypeStruct((B, H // 2, 2 * MLA_V, n_q), BF16),
        scratch_shapes=[pltpu.VMEM((2, 2, tk, tq), F32)],
        compiler_params=_cparams("parallel", "parallel", "parallel"),
        name="mla_attention",
    )(k_chunks, q_t, v_t)


def mla_attend(q, k, v, n_lat, tq=512, tk=768):
    B, H, T, _ = q.shape
    n_ctx = T - n_lat
    q_t = jnp.swapaxes(q, 2, 3)
    v_t = jnp.transpose(v.reshape(B, H // 2, T, 2, MLA_V), (0, 1, 3, 4, 2)).reshape(B, H, MLA_V, T)

    def chunks(rows, size):
        kc = k[:, :, rows].reshape(B, H, -1, size, HEAD_LANES)
        vc = jnp.swapaxes(v_t[:, :, :, rows].reshape(B, H, MLA_V, -1, size), 2, 3)
        return kc, vc

    kc, vc = chunks(slice(0, T), tk)
    att_l = attention(kc, q_t[..., :n_lat], vc, tq)
    kc, vc = chunks(slice(n_lat, T), n_ctx)
    att_c = attention(kc, q_t[..., n_lat:], vc, n_ctx)
    att = jnp.concatenate([att_l, att_c], axis=-1)
    return jnp.transpose(att, (0, 3, 1, 2)).reshape(B, T, H * MLA_V)


def _post_kernel(n_mix, n_lat, tf, *refs):
    x_ref, m_ref, g2_ref = refs[:3]
    mix = refs[3:3 + 2 * n_mix]
    wg_ref, wu_ref, wd_ref, o_ref = refs[3 + 2 * n_mix:]
    b = pl.program_id(0)
    i = pl.program_id(1)
    bm = x_ref.shape[0]
    is_ctx = _ctx_rows(i, bm, n_lat)
    y = None
    for j in range(n_mix):
        t = jnp.dot(mix[2 * j][...], mix[2 * j + 1][...], preferred_element_type=F32)
        y = t if y is None else y + t
    x1 = x_ref[...] + _mod_rows(m_ref, 2, b, is_ctx) * y
    h = _rms(x1, g2_ref[...])
    h = (h * (1.0 + _mod_rows(m_ref, 4, b, is_ctx)) + _mod_rows(m_ref, 3, b, is_ctx)).astype(BF16)
    acc = None
    for f0 in range(0, D_FF, tf):
        g = jnp.dot(h, wg_ref[:, f0:f0 + tf], preferred_element_type=F32)
        u = jnp.dot(h, wu_ref[:, f0:f0 + tf], preferred_element_type=F32)
        a = (g * jax.nn.sigmoid(g) * u).astype(BF16)
        t = jnp.dot(a, wd_ref[f0:f0 + tf, :], preferred_element_type=F32)
        acc = t if acc is None else acc + t
    o_ref[...] = x1 + _mod_rows(m_ref, 5, b, is_ctx) * acc


def post_mixer(x_all, m, g2, mixes, wg, wu, wd, n_lat, n_rows, bm, tf=1408):
    B, T, D = x_all.shape
    row = lambda n: pl.BlockSpec((None, bm, n), lambda b, i: (b, i, 0))
    mix_args, mix_specs = [], []
    for a, wmat in mixes:
        mix_args += [a, wmat]
        mix_specs += [row(a.shape[-1]), _const_spec(wmat.shape, True)]
    return pl.pallas_call(
        functools.partial(_post_kernel, len(mixes), n_lat, tf),
        grid=(B, n_rows // bm),
        in_specs=[row(D), _const_spec(m.shape), _const_spec(g2.shape)] + mix_specs
        + [_const_spec(wg.shape, True), _const_spec(wu.shape, True), _const_spec(wd.shape, True)],
        out_specs=row(D),
        out_shape=jax.ShapeDtypeStruct((B, n_rows, D), F32),
        compiler_params=_cparams("parallel", "parallel"),
        name="post_mixer",
    )(x_all, m, g2, *mix_args, wg, wu, wd)


def _odd_pre_kernel(n_lat, x_ref, m_ref, g1_ref, w_ref, gate_ref, xr_ref):
    b = pl.program_id(0)
    i = pl.program_id(1)
    bm = x_ref.shape[0]
    is_ctx = _ctx_rows(i, bm, n_lat)
    h = _rms(x_ref[...], g1_ref[...])
    h = h * (1.0 + _mod_rows(m_ref, 1, b, is_ctx)) + _mod_rows(m_ref, 0, b, is_ctx)
    u = jnp.dot(h.astype(BF16), w_ref[...], preferred_element_type=F32)
    gate_ref[...] = u[:, :D_RNN].astype(BF16)
    xr_ref[...] = u[:, D_RNN:]


def odd_pre(x_all, m, g1, w_in, n_lat, bm):
    B, T, D = x_all.shape
    row = lambda n: pl.BlockSpec((None, bm, n), lambda b, i: (b, i, 0))
    return pl.pallas_call(
        functools.partial(_odd_pre_kernel, n_lat),
        grid=(B, T // bm),
        in_specs=[row(D), _const_spec(m.shape), _const_spec(g1.shape), _const_spec(w_in.shape)],
        out_specs=[row(D_RNN), row(D_RNN)],
        out_shape=[jax.ShapeDtypeStruct((B, T, D_RNN), BF16), jax.ShapeDtypeStruct((B, T, D_RNN), F32)],
        compiler_params=_cparams("parallel", "parallel"),
        name="odd_pre",
    )(x_all, m, g1, w_in)


def _rglru_kernel(n_lat, chunk, gate_ref, xr_ref, cw_ref, cb_ref, wg_ref, bg_ref, lam_ref, o_ref, hf_ref, xc_ref):
    T = xr_ref.shape[0]
    n_ctx = T - n_lat
    R = chunk
    nv = R // 8
    cw = cw_ref[...]
    cb = cb_ref[...]
    sub = lax.broadcasted_iota(jnp.int32, (nv, 8, RG_BW), 1)

    def conv_chunk(c0, seg_lo, seg_hi):
        lo = jnp.maximum(c0 - 8, 0)
        hi = jnp.minimum(c0 + R, T - 8)
        win = jnp.concatenate([xr_ref[pl.ds(pl.multiple_of(lo, 8), 8), :],
                               xr_ref[pl.ds(pl.multiple_of(c0, 8), R), :],
                               xr_ref[pl.ds(pl.multiple_of(hi, 8), 8), :]], axis=0)
        g = c0 - 8 + lax.broadcasted_iota(jnp.int32, (R + 16, 1), 0)
        win = jnp.where((g >= seg_lo) & (g < seg_hi), win, 0.0)
        y = cb
        for kk in range(RG_CONV):
            off = 8 + kk - RG_CONV // 2
            y = y + win[off:off + R, :] * cw[kk:kk + 1, :]
        return y

    def gates(xc, d):
        pre = jnp.dot(xc.astype(BF16), wg_ref[d], preferred_element_type=F32) + bg_ref[d]
        r = 0.5 * jnp.tanh(0.5 * pre[:, :RG_BW]) + 0.5
        ig = 0.5 * jnp.tanh(0.5 * pre[:, RG_BW:]) + 0.5
        log_a = r * (-RG_C * jax.nn.softplus(-lam_ref[d]))
        a = jnp.exp(log_a)
        bt = jnp.sqrt(-jnp.tanh(log_a) * (a * a + 1.0)) * (ig * xc)
        return a, bt

    def local_scan(a, bt, reverse):
        a = a.reshape(nv, 8, RG_BW)
        bt = bt.reshape(nv, 8, RG_BW)
        for d in (1, 2, 4):
            if reverse:
                keep = sub < 8 - d
                sh = 8 - d
            else:
                keep = sub >= d
                sh = d
            a_s = jnp.where(keep, pltpu.roll(a, sh, 1), 1.0)
            b_s = jnp.where(keep, pltpu.roll(bt, sh, 1), 0.0)
            bt = a * b_s + bt
            a = a * a_s
        return a, bt

    def chunk_scan(a, bt, h, reverse):
        a, bt = local_scan(a, bt, reverse)
        rows = [None] * nv
        order = range(nv - 1, -1, -1) if reverse else range(nv)
        edge = 0 if reverse else 7
        for v in order:
            hv = bt[v] + a[v] * h
            rows[v] = hv
            h = hv[edge:edge + 1, :]
        return jnp.concatenate(rows, axis=0), h

    n_lat_chunks = n_lat // R
    h0 = jnp.zeros((1, RG_BW), F32)

    def fwd_chunk(c0, seg_lo, seg_hi, h):
        xc = conv_chunk(c0, seg_lo, seg_hi)
        a, bt = gates(xc, 0)
        hs, h = chunk_scan(a, bt, h, False)
        rs = pl.ds(pl.multiple_of(c0, 8), R)
        hf_ref[rs, :] = hs
        xc_ref[rs, :] = xc
        return h

    h = h0
    for c in range(n_ctx // R):
        h = fwd_chunk(n_lat + c * R, n_lat, T, h)
    h = lax.fori_loop(0, n_lat_chunks, lambda c, hh: fwd_chunk(c * R, 0, n_lat, hh), h)

    def bwd_chunk(c0, h):
        rs = pl.ds(pl.multiple_of(c0, 8), R)
        a, bt = gates(xc_ref[rs, :], 1)
        hs, h = chunk_scan(a, bt, h, True)
        g = gate_ref[rs, :].astype(F32)
        o_ref[rs, :] = (jax.nn.gelu(g) * (hf_ref[rs, :] + hs)).astype(o_ref.dtype)
        return h

    h = h0
    for c in range(n_ctx // R - 1, -1, -1):
        h = bwd_chunk(n_lat + c * R, h)
    lax.fori_loop(0, n_lat_chunks, lambda c, hh: bwd_chunk((n_lat_chunks - 1 - c) * R, hh), h)


def rglru(gate, xr, cw, cb, wg, bg, lam, n_lat, chunk=256):
    B, T, _ = xr.shape
    col = pl.BlockSpec((None, T, RG_BW), lambda b, n: (b, 0, n))
    return pl.pallas_call(
        functools.partial(_rglru_kernel, n_lat, chunk),
        grid=(B, RG_BLOCKS),
        in_specs=[
            col, col,
            pl.BlockSpec((RG_CONV, RG_BW), lambda b, n: (0, n)),
            pl.BlockSpec((1, RG_BW), lambda b, n: (0, n)),
            pl.BlockSpec((None, 2, RG_BW, 2 * RG_BW), lambda b, n: (n, 0, 0, 0)),
            pl.BlockSpec((None, 2, 1, 2 * RG_BW), lambda b, n: (n, 0, 0, 0)),
            pl.BlockSpec((None, 2, 1, RG_BW), lambda b, n: (n, 0, 0, 0)),
        ],
        out_specs=col,
        out_shape=jax.ShapeDtypeStruct((B, T, D_RNN), BF16),
        scratch_shapes=[pltpu.VMEM((T, RG_BW), F32), pltpu.VMEM((T, RG_BW), F32)],
        compiler_params=_cparams("parallel", "parallel"),
        name="rglru",
    )(gate, xr, cw, cb, wg, bg, lam)


def _hyena_filters(L, w1, b1, w2, b2, w3, freq):
    k = jnp.arange(L, dtype=F32)
    t = (k / max(L - 1, 1))[:, None]
    ang = (2.0 * math.pi / L) * k[:, None] * jnp.linspace(1e-4, HY_BANDS - 1, HY_BANDS, dtype=F32)[None, :]
    z = jnp.concatenate([t, jnp.cos(ang), -jnp.sin(ang)], axis=-1)
    mm = functools.partial(jnp.dot, precision=lax.Precision.HIGHEST)
    h = jnp.sin(freq * (mm(z, w1) + b1))
    h = jnp.sin(freq * (mm(h, w2) + b2))
    h = mm(h, w3).reshape(L, 2, D_HY)
    half = L // 2
    dist = jnp.abs(k - half) / max(half, 1)
    deltas = jnp.abs(jnp.linspace(math.log(HY_TARGET) / HY_SLOW_DECAY, math.log(HY_TARGET) / HY_FAST_DECAY,
                                  D_HY, dtype=F32))
    window = jnp.exp(-dist[:, None] * deltas[None, :])
    return jnp.transpose(h * window[:, None, :], (1, 0, 2))


def _swap_neg(x):
    return jnp.concatenate([x[:, HY_CB:], -x[:, :HY_CB]], axis=1)


def _cstack(x):
    return jnp.concatenate([x, _swap_neg(x)], axis=0)


def _cmul(h, w):
    hre = jnp.concatenate([h[:, :HY_CB], h[:, :HY_CB]], axis=1)
    him = jnp.concatenate([-h[:, HY_CB:], h[:, HY_CB:]], axis=1)
    wsw = jnp.concatenate([w[:, HY_CB:], w[:, :HY_CB]], axis=1)
    return hre * w + him * wsw


def _hy_short_kernel(u_ref, w_ref, b_ref, o_ref):
    L = u_ref.shape[1]
    row = lax.broadcasted_iota(jnp.int32, (L, 1), 0)
    w = w_ref[...]
    halves = []
    for b in range(2):
        u = u_ref[b].astype(F32)
        prev = jnp.where(row >= 1, pltpu.roll(u, 1, 0), 0.0)
        nxt = jnp.where(row < L - 1, pltpu.roll(u, L - 1, 0), 0.0)
        halves.append(prev * w[0:1] + u * w[1:2] + nxt * w[2:3] + b_ref[...])
    o_ref[...] = jnp.concatenate(halves, axis=1).astype(o_ref.dtype)


def hy_short(u_hy, short_w, short_b, row_blk, L):
    return pl.pallas_call(
        _hy_short_kernel,
        grid=(3, HY_NCB),
        in_specs=[
            pl.BlockSpec((2, L, HY_CB), lambda g, cb: (0, row_blk, g * HY_NCB + cb)),
            pl.BlockSpec((3, HY_CB), lambda g, cb: (0, g * HY_NCB + cb)),
            pl.BlockSpec((1, HY_CB), lambda g, cb: (0, g * HY_NCB + cb)),
        ],
        out_specs=pl.BlockSpec((None, None, L, 2 * HY_CB), lambda g, cb: (g, cb, 0, 0)),
        out_shape=jax.ShapeDtypeStruct((3, HY_NCB, L, 2 * HY_CB), BF16),
        compiler_params=_cparams("parallel", "parallel"),
        name="hy_short",
    )(u_hy, short_w, short_b)


def _hy_dft_kernel(gated, x_ref, m_ref, *rest):
    if gated:
        xg_ref, z_ref, b_ref, o_ref = rest
    else:
        (o_ref,) = rest
    w = 2 * HY_CB
    for g in range(m_ref.shape[0]):
        sl = slice(g * w, (g + 1) * w)
        y = jnp.dot(m_ref[g], _cstack(x_ref[:, sl]), preferred_element_type=F32)
        if gated:
            y = xg_ref[:, sl].astype(F32) * (y + b_ref[...] * z_ref[:, sl].astype(F32))
        o_ref[:, sl] = y.astype(o_ref.dtype)


def hy_dft_by_minor(x, mats, gate=None, groups=16):
    ncb, rows, lanes = x.shape
    n_minor, n_out, _ = mats.shape
    w = lanes // n_minor
    blk = lambda r: pl.BlockSpec((None, r, groups * w), lambda cb, s: (cb, 0, s))
    args = [x, mats]
    specs = [blk(rows), pl.BlockSpec((groups, n_out, 2 * rows), lambda cb, s: (s, 0, 0))]
    if gate is not None:
        args += list(gate)
        specs += [blk(n_out), blk(n_out), pl.BlockSpec((None, 1, w), lambda cb, s: (cb, 0, 0))]
    return pl.pallas_call(
        functools.partial(_hy_dft_kernel, gate is not None),
        grid=(ncb, n_minor // groups),
        in_specs=specs,
        out_specs=blk(n_out),
        out_shape=jax.ShapeDtypeStruct((ncb, n_out, lanes), BF16),
        compiler_params=_cparams("parallel", "parallel"),
        name="hy_dft_minor",
    )(*args)


def _hy_mid_kernel(with_filter, a_ref, fb_ref, *rest):
    if with_filter:
        h_ref, fc_ref, o_ref = rest
    else:
        (o_ref,) = rest
    for k in range(a_ref.shape[0]):
        w = jnp.dot(fb_ref[...], _cstack(a_ref[k]), preferred_element_type=F32)
        if with_filter:
            y = _cmul(h_ref[k].astype(F32), w)
            w = jnp.dot(fc_ref[...], _cstack(y.astype(BF16)), preferred_element_type=F32)
        o_ref[k] = w.astype(o_ref.dtype)


def hy_mid(a, fb, spec=None, fc=None, chunk=16):
    ncb, ns, rows, lanes = a.shape
    slab = pl.BlockSpec((None, chunk, rows, lanes), lambda cb, s: (cb, s, 0, 0))
    args, specs = [a, fb], [slab, _const_spec(fb.shape)]
    if spec is not None:
        args += [spec, fc]
        specs += [slab, _const_spec(fc.shape)]
    return pl.pallas_call(
        functools.partial(_hy_mid_kernel, spec is not None),
        grid=(ncb, ns // chunk),
        in_specs=specs,
        out_specs=slab,
        out_shape=jax.ShapeDtypeStruct(a.shape, BF16),
        compiler_params=_cparams("parallel", "parallel"),
        name="hy_mid",
    )(*args)


def _hy_ctx_kernel(s_ref, h_ref, b_ref, ff_ref, fi_ref, o_ref):
    z = s_ref[0].astype(F32)
    L = z.shape[0]
    ff = ff_ref[...]
    for o in range(2):
        hp = jnp.concatenate([h_ref[o], jnp.zeros((L, HY_CB), F32)], axis=1)
        spec = jnp.dot(ff, _cstack(hp.astype(BF16)), preferred_element_type=F32)
        w = jnp.dot(ff, _cstack(z.astype(BF16)), preferred_element_type=F32)
        y = jnp.dot(fi_ref[...], _cstack(_cmul(spec, w).astype(BF16)), preferred_element_type=F32)
        z = s_ref[o + 1].astype(F32) * (y + b_ref[o] * z)
    o_ref[0] = z[:, :HY_CB].astype(o_ref.dtype)
    o_ref[1] = z[:, HY_CB:].astype(o_ref.dtype)


def hy_ctx(s, h, bias_p, ff, fi):
    _, ncb, L, lanes = s.shape
    return pl.pallas_call(
        _hy_ctx_kernel,
        grid=(ncb,),
        in_specs=[
            pl.BlockSpec((3, None, L, lanes), lambda cb: (0, cb, 0, 0)),
            pl.BlockSpec((2, L, HY_CB), lambda cb: (0, 0, cb)),
            pl.BlockSpec((2, None, 1, lanes), lambda cb: (0, cb, 0, 0)),
            _const_spec(ff.shape), _const_spec(fi.shape),
        ],
        out_specs=pl.BlockSpec((2, L, HY_CB), lambda cb: (0, 0, cb)),
        out_shape=jax.ShapeDtypeStruct((2, L, ncb * HY_CB), BF16),
        compiler_params=_cparams("parallel"),
        name="hy_ctx",
    )(s, h, bias_p, ff, fi)


def _hy_tables(L, L_ctx):
    n = 2 * L
    n1 = L // HY_N
    two_pi = 2.0 * math.pi
    ar = lambda m: jnp.arange(m, dtype=jnp.int32)
    cs = lambda m, period: (jnp.cos(m.astype(F32) * (two_pi / period)), jnp.sin(m.astype(F32) * (two_pi / period)))
    m = (ar(HY_N)[None, :, None] * (HY_N * ar(n1)[None, None, :] + ar(HY_N)[:, None, None])) % n
    c, s = cs(m, n)
    ma = jnp.concatenate([c, s], axis=-1).astype(BF16)
    q1 = ar(n1) + n1 // 2
    m = (ar(HY_N)[None, None, :] * (HY_N * q1[None, :, None] + ar(HY_N)[:, None, None])) % n
    c, s = cs(m, n)
    md = (jnp.concatenate([c, -s], axis=-1) / n).astype(BF16)
    m = (ar(HY_N)[:, None] * ar(HY_N)[None, :]) % HY_N
    c, s = cs(m, HY_N)
    fb = jnp.concatenate([c, s], axis=-1).astype(BF16)
    fc = jnp.concatenate([c, -s], axis=-1).astype(BF16)
    nc = 2 * L_ctx
    m = (ar(nc)[:, None] * ar(L_ctx)[None, :]) % nc
    c, s = cs(m, nc)
    ff = jnp.concatenate([c, s], axis=-1).astype(BF16)
    m = ((ar(L_ctx) + L_ctx // 2)[:, None] * ar(nc)[None, :]) % nc
    c, s = cs(m, nc)
    fi = (jnp.concatenate([c, -s], axis=-1) / nc).astype(BF16)
    return {"ma": ma, "md": md, "fb": fb, "fc": fc, "ff": ff, "fi": fi}


def hyena(u_hy, short_w, short_b, filt, bias, tabs, n_lat, n_ctx):
    n1 = n_lat // HY_N
    w = 2 * HY_CB
    bias_p = jnp.concatenate([bias.reshape(2, HY_NCB, 1, HY_CB)] * 2, axis=-1)
    by_minor = lambda t: t.reshape(HY_NCB, -1, HY_N * w)
    slabs = lambda t: t.reshape(HY_NCB, HY_N, HY_N, w)
    h_lat = _hyena_filters(n_lat, *filt)
    spectra = []
    for o in range(2):
        hp = jnp.transpose(h_lat[o].reshape(n_lat, HY_NCB, HY_CB), (1, 0, 2))
        hp = jnp.concatenate([hp, jnp.zeros_like(hp)], axis=-1).astype(BF16)
        a = hy_dft_by_minor(by_minor(hp), tabs["ma"])
        spectra.append(hy_mid(slabs(a), tabs["fb"]))
    s = hy_short(u_hy, short_w, short_b, 0, n_lat)
    z = by_minor(s[0])
    for o in range(2):
        a = hy_dft_by_minor(z, tabs["ma"])
        v = hy_mid(slabs(a), tabs["fb"], spectra[o], tabs["fc"])
        z = hy_dft_by_minor(by_minor(v), tabs["md"], gate=(by_minor(s[o + 1]), z, bias_p[o]))
    z = jnp.transpose(z.reshape(HY_NCB, n_lat, 2, HY_CB), (2, 1, 0, 3)).reshape(2, n_lat, D_HY)
    s_c = hy_short(u_hy, short_w, short_b, n_lat // n_ctx, n_ctx)
    z_c = hy_ctx(s_c, _hyena_filters(n_ctx, *filt), bias_p, tabs["ff"], tabs["fi"])
    return jnp.concatenate([z, z_c], axis=1)


def _rope_tables(n_lat, n_ctx):
    rows = n_lat // GRID_W
    row = jnp.repeat(jnp.arange(rows, dtype=F32), GRID_W)
    col = jnp.tile(jnp.arange(GRID_W, dtype=F32), rows)
    inv_freq = ROPE_BASE ** (-jnp.arange(0, ROPE_AXIS, 2, dtype=F32) / ROPE_AXIS)
    ang = jnp.concatenate([row[:, None] * inv_freq[None, :], col[:, None] * inv_freq[None, :]], axis=-1)
    ang = jnp.concatenate([ang, jnp.zeros((n_ctx, ROPE_AXIS), F32)], axis=0)
    cos, sin = jnp.cos(ang), jnp.sin(ang)
    T = n_lat + n_ctx
    z8 = jnp.zeros((T, ROPE_AXIS // 2), F32)
    cos32 = jnp.concatenate([cos[:, :8], cos[:, :8], cos[:, 8:], cos[:, 8:]], axis=-1)
    first = jnp.concatenate([-sin[:, :8], z8, -sin[:, 8:], z8], axis=-1)
    second = jnp.concatenate([z8, sin[:, :8], z8, sin[:, 8:]], axis=-1)
    pad = lambda t, fill: jnp.concatenate(
        [jnp.full((T, MLA_NOPE), fill, F32), t, jnp.zeros((T, HEAD_LANES - MLA_NOPE - MLA_ROPE), F32)], axis=-1)
    lane = jnp.arange(HEAD_LANES)
    seg = jnp.where(lane < MLA_NOPE, 0, jnp.where(lane < MLA_NOPE + MLA_ROPE, 1, 2))
    msq = (seg[:, None] == seg[None, :]) & (seg[:, None] < 2)
    msq = jnp.where(msq, jnp.where(seg[:, None] == 0, 1.0 / MLA_NOPE, 1.0 / MLA_ROPE), 0.0).astype(BF16)
    return {"ct": pad(cos32, 1.0), "s1": pad(first, 0.0), "s2": pad(second, 0.0), "msq": msq}


def _head_gain(g, scale):
    return jnp.concatenate([g * scale, jnp.zeros((HEAD_LANES - g.shape[0],), F32)])[None, :]


def _even_weights(j, ev_w_in, mla_q_norm_g, mla_w_uq, mla_kv_norm_g, mla_w_ukv, mla_qk_g_q, mla_qk_g_k):
    i_q = 3 * D_HY
    i_kv = i_q + MLA_Q_RANK
    i_kr = i_kv + MLA_KV_RANK
    w_in = ev_w_in[j]
    d = w_in.shape[0]
    kr_group = jnp.concatenate([jnp.zeros((d, MLA_NOPE), F32), w_in[:, i_kr:],
                                jnp.zeros((d, HEAD_LANES - MLA_NOPE - MLA_ROPE), F32)], axis=-1)
    w_in_p = jnp.concatenate([w_in[:, :i_kr], kr_group], axis=-1).astype(BF16)
    hq = MLA_NOPE + MLA_ROPE
    w_uq = mla_w_uq[j].reshape(MLA_Q_RANK, MLA_HEADS, hq)
    w_uq = jnp.pad(w_uq, ((0, 0), (0, 0), (0, HEAD_LANES - hq))).reshape(MLA_Q_RANK, -1).astype(BF16)
    w_ukv = mla_w_ukv[j].reshape(MLA_KV_RANK, MLA_HEADS, MLA_NOPE + MLA_V)
    w_uk = jnp.pad(w_ukv[:, :, :MLA_NOPE], ((0, 0), (0, 0), (0, HEAD_LANES - MLA_NOPE)))
    w_uk = w_uk.reshape(MLA_KV_RANK, -1).astype(BF16)
    w_uv = w_ukv[:, :, MLA_NOPE:].reshape(MLA_KV_RANK, -1).astype(BF16)
    return {
        "w_in": w_in_p, "q_norm_g": mla_q_norm_g[j][None, :], "kv_norm_g": mla_kv_norm_g[j][None, :],
        "w_uq": w_uq, "w_uk": w_uk, "w_uv": w_uv,
        "gq": _head_gain(mla_qk_g_q[j], MLA_SCALE * math.log2(math.e)), "gk": _head_gain(mla_qk_g_k[j], 1.0),
    }


def _rg_gate_weights(j, rg_w_a, rg_b_a, rg_w_x, rg_b_x, rg_lambda):
    wg = jnp.concatenate([rg_w_a[j], rg_w_x[j]], axis=-1)
    wg = jnp.transpose(wg, (1, 0, 2, 3)).astype(BF16)
    blk = lambda t: jnp.transpose(t.reshape(2, RG_BLOCKS, RG_BW), (1, 0, 2))
    bg = jnp.concatenate([blk(rg_b_a[j]), blk(rg_b_x[j])], axis=-1)[:, :, None, :]
    lam = blk(rg_lambda[j])[:, :, None, :]
    return wg, bg, lam


def kernel(x, c, ctx, c_ctx, ada_w, ada_b, norm1_g, norm2_g, ffn_w_in, ffn_w_out, ev_w_in, hy_short_w, hy_short_b, hy_w1, hy_b1, hy_w2, hy_b2, hy_w3, hy_freq, hy_bias, mla_q_norm_g, mla_w_uq, mla_kv_norm_g, mla_w_ukv, mla_qk_g_q, mla_qk_g_k, ev_w_out, rg_w_in, rg_conv_w, rg_conv_b, rg_w_a, rg_b_a, rg_w_x, rg_b_x, rg_lambda, rg_w_out):
    B, n_lat, D = x.shape
    n_ctx = ctx.shape[1]
    T = n_lat + n_ctx
    bm = 768
    assert T % bm == 0 and n_lat % 512 == 0 and n_ctx == 256

    cond8 = jnp.concatenate([c, c_ctx[None, :], jnp.zeros((8 - B - 1, D), F32)], axis=0)
    mods = ada_modulation(cond8, ada_w, ada_b)
    tabs = _rope_tables(n_lat, n_ctx)
    hy_tabs = _hy_tables(n_lat, n_ctx)
    x_all = jnp.concatenate([x, ctx], axis=1)

    for i in range(DEPTH):
        j = i // 2
        last = i == DEPTH - 1
        m = mods[i]
        g1 = norm1_g[i][None, :]
        g2 = norm2_g[i][None, :]
        if i % 2 == 0:
            w = _even_weights(j, ev_w_in, mla_q_norm_g, mla_w_uq, mla_kv_norm_g, mla_w_ukv, mla_qk_g_q, mla_qk_g_k)
            u_hy, q, k, v = even_pre(x_all, m, g1, w, tabs, n_lat, bm)
            att = mla_attend(q, k, v, n_lat)
            filt = (hy_w1[j], hy_b1[j], hy_w2[j], hy_b2[j], hy_w3[j], hy_freq[j])
            hy = hyena(u_hy, hy_short_w[j], hy_short_b[j][None, :], filt, hy_bias[j], hy_tabs, n_lat, n_ctx)
            w_out = ev_w_out[j].astype(BF16)
            mixes = [(hy, w_out[:D_HY]), (att, w_out[D_HY:])]
        else:
            gate, xr = odd_pre(x_all, m, g1, rg_w_in[j].astype(BF16), n_lat, bm)
            wg, bg, lam = _rg_gate_weights(j, rg_w_a, rg_b_a, rg_w_x, rg_b_x, rg_lambda)
            y = rglru(gate, xr, rg_conv_w[j], rg_conv_b[j][None, :], wg, bg, lam, n_lat)
            mixes = [(y, rg_w_out[j].astype(BF16))]
        wg_f = ffn_w_in[i][:, :D_FF].astype(BF16)
        wu_f = ffn_w_in[i][:, D_FF:].astype(BF16)
        wd_f = ffn_w_out[i].astype(BF16)
        if last:
            x_all = post_mixer(x_all, m, g2, mixes, wg_f, wu_f, wd_f, n_lat, n_lat, 512)
        else:
            x_all = post_mixer(x_all, m, g2, mixes, wg_f, wu_f, wd_f, n_lat, T, bm)
    return x_all
```

```python
import functools
import math

import jax
import jax.numpy as jnp
from jax import lax
from jax.experimental import pallas as pl
from jax.experimental.pallas import tpu as pltpu

F32 = jnp.float32
BF16 = jnp.bfloat16
U32 = jnp.uint32

D_MODEL = 1024
DEPTH = 4
GRID_W = 64
D_FF = 2816
NORM_EPS = 1e-6

D_HY = 512
HY_CB = 128
HY_NCB = D_HY // HY_CB
HY_N = 128
HY_PITCH = HY_N + 8
HY_BANDS = 16
HY_FAST_DECAY = 0.3
HY_SLOW_DECAY = 1.5
HY_TARGET = 1e-2

MLA_HEADS = 8
MLA_NOPE = 64
MLA_ROPE = 32
MLA_V = 64
MLA_Q_RANK = 384
MLA_KV_RANK = 256
MLA_SCALE = (MLA_NOPE + MLA_ROPE) ** -0.5
ROPE_AXIS = 16
ROPE_BASE = 10000.0
HEAD_LANES = 128

D_RNN = 1280
RG_BLOCKS = 10
RG_BW = 128
RG_CONV = 4
RG_C = 8.0

EVEN_IN_PAD = 3 * D_HY + MLA_Q_RANK + MLA_KV_RANK + HEAD_LANES

VMEM_LIMIT = 56 * 1024 * 1024


def _cparams(*sem):
    return pltpu.CompilerParams(dimension_semantics=sem, vmem_limit_bytes=VMEM_LIMIT)


def _const_spec(shape, single_buffer=False):
    nd = len(shape)
    kw = {"pipeline_mode": pl.Buffered(1)} if single_buffer else {}
    return pl.BlockSpec(shape, lambda *_: (0,) * nd, **kw)


def _ada_kernel(s_ref, w_ref, b_ref, o_ref):
    s = s_ref[...]
    s = s * jax.nn.sigmoid(s)
    o_ref[...] = jnp.dot(s, w_ref[...], preferred_element_type=F32) + b_ref[...]


def ada_modulation(cond8, ada_w, ada_b):
    depth, d, n = ada_w.shape
    tn = 1536
    return pl.pallas_call(
        _ada_kernel,
        grid=(depth, n // tn),
        in_specs=[
            pl.BlockSpec((8, d), lambda i, j: (0, 0)),
            pl.BlockSpec((None, d, tn), lambda i, j: (i, 0, j)),
            pl.BlockSpec((None, 1, tn), lambda i, j: (i, 0, j)),
        ],
        out_specs=pl.BlockSpec((None, 8, tn), lambda i, j: (i, 0, j)),
        out_shape=jax.ShapeDtypeStruct((depth, 8, n), F32),
        compiler_params=_cparams("parallel", "parallel"),
        name="ada_modulation",
    )(cond8, ada_w, ada_b.reshape(depth, 1, n))


def _mod_rows(m_ref, idx, b, is_ctx):
    d = D_MODEL
    lat = m_ref[pl.ds(b, 1), pl.ds(idx * d, d)]
    ctx = m_ref[pl.ds(2, 1), pl.ds(idx * d, d)]
    return jnp.where(is_ctx, ctx, lat)


def _rms(x, g):
    ms = jnp.mean(x * x, axis=-1, keepdims=True)
    return x * lax.rsqrt(ms + NORM_EPS) * g


def _ctx_rows(i, bm, n_lat):
    row = i * bm + lax.broadcasted_iota(jnp.int32, (bm, 1), 0)
    return row >= n_lat


def _even_pre_kernel(n_lat, x_ref, m_ref, g1_ref, w_in_ref, qg_ref, kvg_ref, w_uq_ref, w_uk_ref,
                     w_uv_ref, gq_ref, gk_ref, msq_ref, ct_ref, s1_ref, s2_ref,
                     hy_ref, q_ref, k_ref, v_ref):
    b = pl.program_id(0)
    i = pl.program_id(1)
    bm = x_ref.shape[0]
    is_ctx = _ctx_rows(i, bm, n_lat)
    x = x_ref[...]
    h = _rms(x, g1_ref[...])
    h = h * (1.0 + _mod_rows(m_ref, 1, b, is_ctx)) + _mod_rows(m_ref, 0, b, is_ctx)
    u = jnp.dot(h.astype(BF16), w_in_ref[...], preferred_element_type=F32)
    i_q = 3 * D_HY
    i_kv = i_q + MLA_Q_RANK
    i_kr = i_kv + MLA_KV_RANK
    hy_ref[...] = u[:, :i_q].astype(BF16)

    ct = ct_ref[...]
    s1 = s1_ref[...]
    s2 = s2_ref[...]
    msq = msq_ref[...]

    def head_norm(t, g):
        ms = jnp.dot((t * t).astype(BF16), msq, preferred_element_type=F32)
        return t * lax.rsqrt(ms + NORM_EPS) * g

    def rope(t):
        return (t * ct + pltpu.roll(t, HEAD_LANES - ROPE_AXIS // 2, 1) * s1
                + pltpu.roll(t, ROPE_AXIS // 2, 1) * s2)

    qn = _rms(u[:, i_q:i_kv], qg_ref[...]).astype(BF16)
    kvn = _rms(u[:, i_kv:i_kr], kvg_ref[...]).astype(BF16)
    gq = gq_ref[...]
    gk = gk_ref[...]
    k_rope = rope(head_norm(u[:, i_kr:], gk))
    for hd in range(MLA_HEADS):
        sl = slice(hd * HEAD_LANES, (hd + 1) * HEAD_LANES)
        qh = jnp.dot(qn, w_uq_ref[:, sl], preferred_element_type=F32)
        q_ref[hd] = rope(head_norm(qh, gq)).T.astype(BF16)
        kh = jnp.dot(kvn, w_uk_ref[:, sl], preferred_element_type=F32)
        k_ref[hd] = (head_norm(kh, gk) + k_rope).astype(BF16)
    v_t = jnp.dot(kvn, w_uv_ref[...], preferred_element_type=F32).T
    for hd in range(MLA_HEADS):
        v_ref[hd] = v_t[hd * MLA_V:(hd + 1) * MLA_V, :].astype(BF16)


def even_pre(x_all, m, g1, w, tabs, n_lat, bm):
    B, T, D = x_all.shape
    H = MLA_HEADS
    row = lambda n: pl.BlockSpec((None, bm, n), lambda b, i: (b, i, 0))
    tab = pl.BlockSpec((bm, HEAD_LANES), lambda b, i: (i, 0))
    headed = lambda nh: pl.BlockSpec((None, nh, bm, HEAD_LANES), lambda b, i: (b, 0, i, 0))
    consts = [m, g1, w["w_in"], w["q_norm_g"], w["kv_norm_g"], w["w_uq"], w["w_uk"], w["w_uv"],
              w["gq"], w["gk"], tabs["msq"]]
    return pl.pallas_call(
        functools.partial(_even_pre_kernel, n_lat),
        grid=(B, T // bm),
        in_specs=[row(D)] + [_const_spec(c.shape) for c in consts] + [tab, tab, tab],
        out_specs=[
            row(3 * D_HY),
            pl.BlockSpec((None, H, HEAD_LANES, bm), lambda b, i: (b, 0, 0, i)),
            headed(H),
            pl.BlockSpec((None, H, None, MLA_V, bm), lambda b, i: (b, 0, i, 0, 0)),
        ],
        out_shape=[
            jax.ShapeDtypeStruct((B, T, 3 * D_HY), BF16),
            jax.ShapeDtypeStruct((B, H, HEAD_LANES, T), BF16),
            jax.ShapeDtypeStruct((B, H, T, HEAD_LANES), BF16),
            jax.ShapeDtypeStruct((B, H, T // bm, MLA_V, bm), BF16),
        ],
        compiler_params=_cparams("parallel", "parallel"),
        name="even_pre",
    )(x_all, *consts, tabs["ct"], tabs["s1"], tabs["s2"])


def _attn_kernel(k_ref, qt_ref, vt_ref, o_ref, s_ref):
    n_chunks = k_ref.shape[1]
    tq = qt_ref.shape[2]
    assert n_chunks % 2 == 1

    def scores(c, slot):
        for hh in range(2):
            s_ref[slot, hh] = jnp.dot(k_ref[hh, c], qt_ref[hh], preferred_element_type=F32)

    def absorb(c, slot, stats):
        out = []
        for hh in range(2):
            m, l, acc = stats[hh]
            m_new = jnp.maximum(m, jnp.max(s_ref[slot, hh], axis=0, keepdims=True))
            alpha = jnp.exp2(m - m_new)
            p = jnp.exp2(s_ref[slot, hh] - m_new)
            l = alpha * l + jnp.sum(p, axis=0, keepdims=True)
            acc = alpha * acc + jnp.dot(vt_ref[hh, c], p.astype(BF16), preferred_element_type=F32)
            out.append((m_new, l, acc))
        return tuple(out)

    def body(i, stats):
        c = 2 * i
        scores(c + 1, 1)
        stats = absorb(c, 0, stats)
        scores(c + 2, 0)
        return absorb(c + 1, 1, stats)

    init = tuple((jnp.full((1, tq), -jnp.inf, F32), jnp.zeros((1, tq), F32), jnp.zeros((MLA_V, tq), F32))
                 for _ in range(2))
    scores(0, 0)
    stats = lax.fori_loop(0, n_chunks // 2, body, init)
    res = absorb(n_chunks - 1, 0, stats)
    out_t = jnp.concatenate([acc / l for _, l, acc in res], axis=0)
    o_ref[...] = out_t.T.astype(o_ref.dtype)


def attention(k_chunks, q_t, v_t, q_blk0, n_q, tq):
    B, H, n_chunks, tk, _ = k_chunks.shape
    return pl.pallas_call(
        _attn_kernel,
        grid=(B, H // 2, n_q // tq),
        in_specs=[
            pl.BlockSpec((None, 2, n_chunks, tk, HEAD_LANES), lambda b, h, i: (b, h, 0, 0, 0)),
            pl.BlockSpec((None, 2, HEAD_LANES, tq), lambda b, h, i: (b, h, 0, q_blk0 + i)),
            pl.BlockSpec((None, 2, n_chunks, MLA_V, tk), lambda b, h, i: (b, h, 0, 0, 0)),
        ],
        out_specs=pl.BlockSpec((None, tq, 2 * MLA_V), lambda b, h, i: (b, i, h)),
        out_shape=jax.ShapeDtypeStruct((B, n_q, H * MLA_V), BF16),
        scratch_shapes=[pltpu.VMEM((2, 2, tk, tq), F32)],
        compiler_params=_cparams("parallel", "parallel", "parallel"),
        name="mla_attention",
    )(k_chunks, q_t, v_t)


def mla_attend(q_t, k, v_t, n_lat, tq=512):
    B, H, T, _ = k.shape
    n_chunks, tk = v_t.shape[2], v_t.shape[4]
    n_ctx = T - n_lat
    att_l = attention(k.reshape(B, H, n_chunks, tk, HEAD_LANES), q_t, v_t, 0, n_lat, tq)
    k_c = k[:, :, n_lat:].reshape(B, H, 1, n_ctx, HEAD_LANES)
    v_c = v_t[:, :, n_chunks - 1, :, tk - n_ctx:].reshape(B, H, 1, MLA_V, n_ctx)
    att_c = attention(k_c, q_t, v_c, n_lat // n_ctx, n_ctx, n_ctx)
    return jnp.concatenate([att_l, att_c], axis=1)


def _post_kernel(n_mix, n_lat, tf, *refs):
    x_ref, m_ref, g2_ref = refs[:3]
    mix = refs[3:3 + 2 * n_mix]
    wg_ref, wu_ref, wd_ref, o_ref = refs[3 + 2 * n_mix:]
    b = pl.program_id(0)
    i = pl.program_id(1)
    bm = x_ref.shape[0]
    is_ctx = _ctx_rows(i, bm, n_lat)
    y = None
    for j in range(n_mix):
        t = jnp.dot(mix[2 * j][...], mix[2 * j + 1][...], preferred_element_type=F32)
        y = t if y is None else y + t
    x1 = x_ref[...] + _mod_rows(m_ref, 2, b, is_ctx) * y
    h = _rms(x1, g2_ref[...])
    h = (h * (1.0 + _mod_rows(m_ref, 4, b, is_ctx)) + _mod_rows(m_ref, 3, b, is_ctx)).astype(BF16)
    acc = None
    for f0 in range(0, D_FF, tf):
        g = jnp.dot(h, wg_ref[:, f0:f0 + tf], preferred_element_type=F32)
        u = jnp.dot(h, wu_ref[:, f0:f0 + tf], preferred_element_type=F32)
        a = (g * jax.nn.sigmoid(g) * u).astype(BF16)
        t = jnp.dot(a, wd_ref[f0:f0 + tf, :], preferred_element_type=F32)
        acc = t if acc is None else acc + t
    o_ref[...] = x1 + _mod_rows(m_ref, 5, b, is_ctx) * acc


def post_mixer(x_all, m, g2, mixes, wg, wu, wd, n_lat, n_rows, bm, tf=1408):
    B, T, D = x_all.shape
    row = lambda n: pl.BlockSpec((None, bm, n), lambda b, i: (b, i, 0))
    mix_args, mix_specs = [], []
    for a, wmat in mixes:
        mix_args += [a, wmat]
        mix_specs += [row(a.shape[-1]), _const_spec(wmat.shape, True)]
    return pl.pallas_call(
        functools.partial(_post_kernel, len(mixes), n_lat, tf),
        grid=(B, n_rows // bm),
        in_specs=[row(D), _const_spec(m.shape), _const_spec(g2.shape)] + mix_specs
        + [_const_spec(wg.shape, True), _const_spec(wu.shape, True), _const_spec(wd.shape, True)],
        out_specs=row(D),
        out_shape=jax.ShapeDtypeStruct((B, n_rows, D), F32),
        compiler_params=_cparams("parallel", "parallel"),
        name="post_mixer",
    )(x_all, m, g2, *mix_args, wg, wu, wd)


def _odd_pre_kernel(n_lat, x_ref, m_ref, g1_ref, w_ref, gate_ref, xr_ref):
    b = pl.program_id(0)
    i = pl.program_id(1)
    bm = x_ref.shape[0]
    is_ctx = _ctx_rows(i, bm, n_lat)
    h = _rms(x_ref[...], g1_ref[...])
    h = h * (1.0 + _mod_rows(m_ref, 1, b, is_ctx)) + _mod_rows(m_ref, 0, b, is_ctx)
    u = jnp.dot(h.astype(BF16), w_ref[...], preferred_element_type=F32)
    gate_ref[...] = u[:, :D_RNN].astype(BF16)
    xr_ref[...] = u[:, D_RNN:]


def odd_pre(x_all, m, g1, w_in, n_lat, bm):
    B, T, D = x_all.shape
    row = lambda n: pl.BlockSpec((None, bm, n), lambda b, i: (b, i, 0))
    return pl.pallas_call(
        functools.partial(_odd_pre_kernel, n_lat),
        grid=(B, T // bm),
        in_specs=[row(D), _const_spec(m.shape), _const_spec(g1.shape), _const_spec(w_in.shape)],
        out_specs=[row(D_RNN), row(D_RNN)],
        out_shape=[jax.ShapeDtypeStruct((B, T, D_RNN), BF16), jax.ShapeDtypeStruct((B, T, D_RNN), F32)],
        compiler_params=_cparams("parallel", "parallel"),
        name="odd_pre",
    )(x_all, m, g1, w_in)


def _rglru_kernel(n_lat, chunk, gate_ref, xr_ref, cw_ref, cb_ref, wg_ref, bg_ref, lam_ref, o_ref, hf_ref, xc_ref):
    T = xr_ref.shape[0]
    n_ctx = T - n_lat
    R = chunk
    nv = R // 8
    cw = cw_ref[...]
    cb = cb_ref[...]
    sub = lax.broadcasted_iota(jnp.int32, (nv, 8, RG_BW), 1)

    def conv_chunk(c0, seg_lo, seg_hi):
        lo = jnp.maximum(c0 - 8, 0)
        hi = jnp.minimum(c0 + R, T - 8)
        win = jnp.concatenate([xr_ref[pl.ds(pl.multiple_of(lo, 8), 8), :],
                               xr_ref[pl.ds(pl.multiple_of(c0, 8), R), :],
                               xr_ref[pl.ds(pl.multiple_of(hi, 8), 8), :]], axis=0)
        g = c0 - 8 + lax.broadcasted_iota(jnp.int32, (R + 16, 1), 0)
        win = jnp.where((g >= seg_lo) & (g < seg_hi), win, 0.0)
        y = cb
        for kk in range(RG_CONV):
            off = 8 + kk - RG_CONV // 2
            y = y + win[off:off + R, :] * cw[kk:kk + 1, :]
        return y

    def gates(xc, d):
        pre = jnp.dot(xc.astype(BF16), wg_ref[d], preferred_element_type=F32) + bg_ref[d]
        r = 0.5 * jnp.tanh(0.5 * pre[:, :RG_BW]) + 0.5
        ig = 0.5 * jnp.tanh(0.5 * pre[:, RG_BW:]) + 0.5
        log_a = r * (-RG_C * jax.nn.softplus(-lam_ref[d]))
        a = jnp.exp(log_a)
        bt = jnp.sqrt(-jnp.tanh(log_a) * (a * a + 1.0)) * (ig * xc)
        return a, bt

    def local_scan(a, bt, reverse):
        a = a.reshape(nv, 8, RG_BW)
        bt = bt.reshape(nv, 8, RG_BW)
        for d in (1, 2, 4):
            if reverse:
                keep = sub < 8 - d
                sh = 8 - d
            else:
                keep = sub >= d
                sh = d
            a_s = jnp.where(keep, pltpu.roll(a, sh, 1), 1.0)
            b_s = jnp.where(keep, pltpu.roll(bt, sh, 1), 0.0)
            bt = a * b_s + bt
            a = a * a_s
        return a, bt

    def chunk_scan(a, bt, h, reverse):
        a, bt = local_scan(a, bt, reverse)
        rows = [None] * nv
        order = range(nv - 1, -1, -1) if reverse else range(nv)
        edge = 0 if reverse else 7
        for v in order:
            hv = bt[v] + a[v] * h
            rows[v] = hv
            h = hv[edge:edge + 1, :]
        return jnp.concatenate(rows, axis=0), h

    n_lat_chunks = n_lat // R
    h0 = jnp.zeros((1, RG_BW), F32)

    def fwd_chunk(c0, seg_lo, seg_hi, h):
        xc = conv_chunk(c0, seg_lo, seg_hi)
        a, bt = gates(xc, 0)
        hs, h = chunk_scan(a, bt, h, False)
        rs = pl.ds(pl.multiple_of(c0, 8), R)
        hf_ref[rs, :] = hs
        xc_ref[rs, :] = xc
        return h

    h = h0
    for c in range(n_ctx // R):
        h = fwd_chunk(n_lat + c * R, n_lat, T, h)
    h = lax.fori_loop(0, n_lat_chunks, lambda c, hh: fwd_chunk(c * R, 0, n_lat, hh), h)

    def bwd_chunk(c0, h):
        rs = pl.ds(pl.multiple_of(c0, 8), R)
        a, bt = gates(xc_ref[rs, :], 1)
        hs, h = chunk_scan(a, bt, h, True)
        g = gate_ref[rs, :].astype(F32)
        o_ref[rs, :] = (jax.nn.gelu(g) * (hf_ref[rs, :] + hs)).astype(o_ref.dtype)
        return h

    h = h0
    for c in range(n_ctx // R - 1, -1, -1):
        h = bwd_chunk(n_lat + c * R, h)
    lax.fori_loop(0, n_lat_chunks, lambda c, hh: bwd_chunk((n_lat_chunks - 1 - c) * R, hh), h)


def rglru(gate, xr, cw, cb, wg, bg, lam, n_lat, chunk=256):
    B, T, _ = xr.shape
    col = pl.BlockSpec((None, T, RG_BW), lambda b, n: (b, 0, n))
    return pl.pallas_call(
        functools.partial(_rglru_kernel, n_lat, chunk),
        grid=(B, RG_BLOCKS),
        in_specs=[
            col, col,
            pl.BlockSpec((RG_CONV, RG_BW), lambda b, n: (0, n)),
            pl.BlockSpec((1, RG_BW), lambda b, n: (0, n)),
            pl.BlockSpec((None, 2, RG_BW, 2 * RG_BW), lambda b, n: (n, 0, 0, 0)),
            pl.BlockSpec((None, 2, 1, 2 * RG_BW), lambda b, n: (n, 0, 0, 0)),
            pl.BlockSpec((None, 2, 1, RG_BW), lambda b, n: (n, 0, 0, 0)),
        ],
        out_specs=col,
        out_shape=jax.ShapeDtypeStruct((B, T, D_RNN), BF16),
        scratch_shapes=[pltpu.VMEM((T, RG_BW), F32), pltpu.VMEM((T, RG_BW), F32)],
        compiler_params=_cparams("parallel", "parallel"),
        name="rglru",
    )(gate, xr, cw, cb, wg, bg, lam)


def _hyena_filters(L, w1, b1, w2, b2, w3, freq):
    k = jnp.arange(L, dtype=F32)
    t = (k / max(L - 1, 1))[:, None]
    ang = (2.0 * math.pi / L) * k[:, None] * jnp.linspace(1e-4, HY_BANDS - 1, HY_BANDS, dtype=F32)[None, :]
    z = jnp.concatenate([t, jnp.cos(ang), -jnp.sin(ang)], axis=-1)
    mm = functools.partial(jnp.dot, precision=lax.Precision.HIGHEST)
    h = jnp.sin(freq * (mm(z, w1) + b1))
    h = jnp.sin(freq * (mm(h, w2) + b2))
    h = mm(h, w3).reshape(L, 2, D_HY)
    half = L // 2
    dist = jnp.abs(k - half) / max(half, 1)
    deltas = jnp.abs(jnp.linspace(math.log(HY_TARGET) / HY_SLOW_DECAY, math.log(HY_TARGET) / HY_FAST_DECAY,
                                  D_HY, dtype=F32))
    window = jnp.exp(-dist[:, None] * deltas[None, :])
    return jnp.transpose(h * window[:, None, :], (1, 0, 2))


def _swap_neg(x):
    return jnp.concatenate([x[:, HY_CB:], -x[:, :HY_CB]], axis=1)


def _cstack(x):
    return jnp.concatenate([x, _swap_neg(x)], axis=0)


def _cmul(h, w):
    hre = jnp.concatenate([h[:, :HY_CB], h[:, :HY_CB]], axis=1)
    him = jnp.concatenate([-h[:, HY_CB:], h[:, HY_CB:]], axis=1)
    wsw = jnp.concatenate([w[:, HY_CB:], w[:, :HY_CB]], axis=1)
    return hre * w + him * wsw


def _hy_short_kernel(u_ref, w_ref, b_ref, o_ref):
    L = u_ref.shape[1]
    row = lax.broadcasted_iota(jnp.int32, (L, 1), 0)
    w = w_ref[...]
    halves = []
    for b in range(2):
        u = u_ref[b].astype(F32)
        prev = jnp.where(row >= 1, pltpu.roll(u, 1, 0), 0.0)
        nxt = jnp.where(row < L - 1, pltpu.roll(u, L - 1, 0), 0.0)
        halves.append(prev * w[0:1] + u * w[1:2] + nxt * w[2:3] + b_ref[...])
    o_ref[...] = jnp.concatenate(halves, axis=1).astype(o_ref.dtype)


def hy_short(u_hy, short_w, short_b, row_blk, L):
    return pl.pallas_call(
        _hy_short_kernel,
        grid=(3, HY_NCB),
        in_specs=[
            pl.BlockSpec((2, L, HY_CB), lambda g, cb: (0, row_blk, g * HY_NCB + cb)),
            pl.BlockSpec((3, HY_CB), lambda g, cb: (0, g * HY_NCB + cb)),
            pl.BlockSpec((1, HY_CB), lambda g, cb: (0, g * HY_NCB + cb)),
        ],
        out_specs=pl.BlockSpec((None, None, L, 2 * HY_CB), lambda g, cb: (g, cb, 0, 0)),
        out_shape=jax.ShapeDtypeStruct((3, HY_NCB, L, 2 * HY_CB), BF16),
        compiler_params=_cparams("parallel", "parallel"),
        name="hy_short",
    )(u_hy, short_w, short_b)


def _hy_conv_kernel(with_filter, z_ref, ma_ref, fb_ref, *rest):
    if with_filter:
        h_ref, fc_ref, md_ref, o_ref, work = rest
    else:
        o_ref, work = rest
    n1 = z_ref.shape[0] // HY_N
    unroll = 8

    def put(rows, val):
        work[0, rows, :] = val[:, :HY_CB]
        work[1, rows, :] = val[:, HY_CB:]

    def get(rows):
        return jnp.concatenate([work[0, rows, :], work[1, rows, :]], axis=1).astype(BF16)

    def stage_a(i, carry):
        for g in range(unroll):
            j2 = i * unroll + g
            x = z_ref[pl.ds(pl.multiple_of(j2 * n1, 16), n1), :]
            a = jnp.dot(ma_ref[j2], _cstack(x), preferred_element_type=F32)
            put(pl.ds(j2, HY_N, stride=HY_PITCH), a)
        return carry

    lax.fori_loop(0, HY_N // unroll, stage_a, 0)

    def mid(i, carry):
        for g in range(unroll):
            k1 = i * unroll + g
            slab = pl.ds(pl.multiple_of(k1 * HY_PITCH, 8), HY_N)
            rows = pl.ds(pl.multiple_of(k1 * HY_N, 16), HY_N)
            w = jnp.dot(fb_ref[...], _cstack(get(slab)), preferred_element_type=F32)
            if with_filter:
                y = _cmul(h_ref[rows, :].astype(F32), w)
                put(slab, jnp.dot(fc_ref[...], _cstack(y.astype(BF16)), preferred_element_type=F32))
            else:
                o_ref[rows, :] = w.astype(o_ref.dtype)
        return carry

    lax.fori_loop(0, HY_N // unroll, mid, 0)
    if not with_filter:
        return

    def stage_d(i, carry):
        for g in range(unroll):
            q2 = i * unroll + g
            y = jnp.dot(md_ref[q2], _cstack(get(pl.ds(q2, HY_N, stride=HY_PITCH))), preferred_element_type=F32)
            o_ref[pl.ds(pl.multiple_of(q2 * n1, 16), n1), :] = y.astype(o_ref.dtype)
        return carry

    lax.fori_loop(0, HY_N // unroll, stage_d, 0)


def hy_conv(z_t, tabs, spectrum=None):
    ncb, L, lanes = z_t.shape
    once = {"pipeline_mode": pl.Buffered(1)}
    blk = lambda rows: pl.BlockSpec((None, rows, lanes), lambda cb: (cb, 0, 0), **once)
    args = [z_t, tabs["ma"], tabs["fb"]]
    specs = [blk(L), _const_spec(tabs["ma"].shape, True), _const_spec(tabs["fb"].shape, True)]
    n_out = HY_N * HY_N
    if spectrum is not None:
        args += [spectrum, tabs["fc"], tabs["md"]]
        specs += [blk(HY_N * HY_N), _const_spec(tabs["fc"].shape, True), _const_spec(tabs["md"].shape, True)]
        n_out = L
    return pl.pallas_call(
        functools.partial(_hy_conv_kernel, spectrum is not None),
        grid=(ncb,),
        in_specs=specs,
        out_specs=pl.BlockSpec((None, n_out, lanes), lambda cb: (cb, 0, 0)),
        out_shape=jax.ShapeDtypeStruct((ncb, n_out, lanes), BF16),
        scratch_shapes=[pltpu.VMEM((2, HY_N * HY_PITCH, HY_CB), F32)],
        compiler_params=_cparams("parallel"),
        name="hy_conv",
    )(*args)


def _hy_gate_kernel(unpack, x_ref, y_ref, z_ref, b_ref, o_ref):
    z = x_ref[...].astype(F32) * (y_ref[...].astype(F32) + b_ref[...] * z_ref[...].astype(F32))
    if unpack:
        o_ref[0] = z[:, :HY_CB].astype(o_ref.dtype)
        o_ref[1] = z[:, HY_CB:].astype(o_ref.dtype)
    else:
        o_ref[...] = z.astype(o_ref.dtype)


def hy_gate(x, y, z, bias_p, unpack, rows=2048):
    ncb, L, lanes = x.shape
    blk = pl.BlockSpec((None, rows, lanes), lambda cb, r: (cb, r, 0))
    if unpack:
        out_spec = pl.BlockSpec((2, rows, HY_CB), lambda cb, r: (0, r, cb))
        out_shape = jax.ShapeDtypeStruct((2, L, ncb * HY_CB), BF16)
    else:
        out_spec, out_shape = blk, jax.ShapeDtypeStruct(x.shape, BF16)
    return pl.pallas_call(
        functools.partial(_hy_gate_kernel, unpack),
        grid=(ncb, L // rows),
        in_specs=[blk, blk, blk, pl.BlockSpec((None, 1, lanes), lambda cb, r: (cb, 0, 0))],
        out_specs=out_spec,
        out_shape=out_shape,
        compiler_params=_cparams("parallel", "parallel"),
        name="hy_gate",
    )(x, y, z, bias_p)


def _hy_ctx_kernel(s_ref, h_ref, b_ref, ff_ref, fi_ref, o_ref):
    z = s_ref[0].astype(F32)
    L = z.shape[0]
    ff = ff_ref[...]
    for o in range(2):
        hp = jnp.concatenate([h_ref[o], jnp.zeros((L, HY_CB), F32)], axis=1)
        spec = jnp.dot(ff, _cstack(hp.astype(BF16)), preferred_element_type=F32)
        w = jnp.dot(ff, _cstack(z.astype(BF16)), preferred_element_type=F32)
        y = jnp.dot(fi_ref[...], _cstack(_cmul(spec, w).astype(BF16)), preferred_element_type=F32)
        z = s_ref[o + 1].astype(F32) * (y + b_ref[o] * z)
    o_ref[0] = z[:, :HY_CB].astype(o_ref.dtype)
    o_ref[1] = z[:, HY_CB:].astype(o_ref.dtype)


def hy_ctx(s, h, bias_p, ff, fi):
    _, ncb, L, _ = s.shape
    return pl.pallas_call(
        _hy_ctx_kernel,
        grid=(ncb,),
        in_specs=[
            pl.BlockSpec((3, None, L, 2 * HY_CB), lambda cb: (0, cb, 0, 0)),
            pl.BlockSpec((2, L, HY_CB), lambda cb: (0, 0, cb)),
            pl.BlockSpec((2, None, 1, 2 * HY_CB), lambda cb: (0, cb, 0, 0)),
            _const_spec(ff.shape), _const_spec(fi.shape),
        ],
        out_specs=pl.BlockSpec((2, L, HY_CB), lambda cb: (0, 0, cb)),
        out_shape=jax.ShapeDtypeStruct((2, L, ncb * HY_CB), BF16),
        compiler_params=_cparams("parallel"),
        name="hy_ctx",
    )(s, h, bias_p, ff, fi)


def _hy_tables(L, L_ctx):
    n = 2 * L
    n1 = L // HY_N
    two_pi = 2.0 * math.pi
    ar = lambda m: jnp.arange(m, dtype=jnp.int32)
    cs = lambda m, period: (jnp.cos(m.astype(F32) * (two_pi / period)), jnp.sin(m.astype(F32) * (two_pi / period)))
    m = (ar(HY_N)[None, :, None] * (HY_N * ar(n1)[None, None, :] + ar(HY_N)[:, None, None])) % n
    c, s = cs(m, n)
    ma = jnp.concatenate([c, s], axis=-1).astype(BF16)
    q1 = ar(n1) + n1 // 2
    m = (ar(HY_N)[None, None, :] * (HY_N * q1[None, :, None] + ar(HY_N)[:, None, None])) % n
    c, s = cs(m, n)
    md = (jnp.concatenate([c, -s], axis=-1) / n).astype(BF16)
    m = (ar(HY_N)[:, None] * ar(HY_N)[None, :]) % HY_N
    c, s = cs(m, HY_N)
    fb = jnp.concatenate([c, s], axis=-1).astype(BF16)
    fc = jnp.concatenate([c, -s], axis=-1).astype(BF16)
    nc = 2 * L_ctx
    m = (ar(nc)[:, None] * ar(L_ctx)[None, :]) % nc
    c, s = cs(m, nc)
    ff = jnp.concatenate([c, s], axis=-1).astype(BF16)
    m = ((ar(L_ctx) + L_ctx // 2)[:, None] * ar(nc)[None, :]) % nc
    c, s = cs(m, nc)
    fi = (jnp.concatenate([c, -s], axis=-1) / nc).astype(BF16)
    return {"ma": ma, "md": md, "fb": fb, "fc": fc, "ff": ff, "fi": fi}


def hyena(u_hy, short_w, short_b, filt, bias, tabs, n_lat, n_ctx):
    bias_p = jnp.concatenate([bias.reshape(2, HY_NCB, 1, HY_CB)] * 2, axis=-1)
    n1 = n_lat // HY_N
    to_slabs = lambda t: jnp.swapaxes(t.reshape(HY_NCB, n1, HY_N, -1), 1, 2).reshape(HY_NCB, n_lat, -1)
    to_time = lambda t: jnp.swapaxes(t.reshape(HY_NCB, HY_N, n1, -1), 1, 2).reshape(HY_NCB, n_lat, -1)
    h_lat = _hyena_filters(n_lat, *filt)
    h_lat = jnp.transpose(h_lat.reshape(2, n1, HY_N, HY_NCB, HY_CB), (0, 3, 2, 1, 4)).reshape(2, HY_NCB, n_lat, HY_CB)
    h_lat = jnp.concatenate([h_lat, jnp.zeros_like(h_lat)], axis=-1).astype(BF16)
    s = hy_short(u_hy, short_w, short_b, 0, n_lat)
    z = s[0]
    for o in range(2):
        y = hy_conv(to_slabs(z), tabs, hy_conv(h_lat[o], tabs))
        z = hy_gate(s[o + 1], to_time(y), z, bias_p[o], o == 1)
    s_c = hy_short(u_hy, short_w, short_b, n_lat // n_ctx, n_ctx)
    z_c = hy_ctx(s_c, _hyena_filters(n_ctx, *filt), bias_p, tabs["ff"], tabs["fi"])
    return jnp.concatenate([z, z_c], axis=1)


def _rope_tables(n_lat, n_ctx):
    rows = n_lat // GRID_W
    row = jnp.repeat(jnp.arange(rows, dtype=F32), GRID_W)
    col = jnp.tile(jnp.arange(GRID_W, dtype=F32), rows)
    inv_freq = ROPE_BASE ** (-jnp.arange(0, ROPE_AXIS, 2, dtype=F32) / ROPE_AXIS)
    ang = jnp.concatenate([row[:, None] * inv_freq[None, :], col[:, None] * inv_freq[None, :]], axis=-1)
    ang = jnp.concatenate([ang, jnp.zeros((n_ctx, ROPE_AXIS), F32)], axis=0)
    cos, sin = jnp.cos(ang), jnp.sin(ang)
    T = n_lat + n_ctx
    z8 = jnp.zeros((T, ROPE_AXIS // 2), F32)
    cos32 = jnp.concatenate([cos[:, :8], cos[:, :8], cos[:, 8:], cos[:, 8:]], axis=-1)
    first = jnp.concatenate([-sin[:, :8], z8, -sin[:, 8:], z8], axis=-1)
    second = jnp.concatenate([z8, sin[:, :8], z8, sin[:, 8:]], axis=-1)
    pad = lambda t, fill: jnp.concatenate(
        [jnp.full((T, MLA_NOPE), fill, F32), t, jnp.zeros((T, HEAD_LANES - MLA_NOPE - MLA_ROPE), F32)], axis=-1)
    lane = jnp.arange(HEAD_LANES)
    seg = jnp.where(lane < MLA_NOPE, 0, jnp.where(lane < MLA_NOPE + MLA_ROPE, 1, 2))
    msq = (seg[:, None] == seg[None, :]) & (seg[:, None] < 2)
    msq = jnp.where(msq, jnp.where(seg[:, None] == 0, 1.0 / MLA_NOPE, 1.0 / MLA_ROPE), 0.0).astype(BF16)
    return {"ct": pad(cos32, 1.0), "s1": pad(first, 0.0), "s2": pad(second, 0.0), "msq": msq}


def _head_gain(g, scale):
    return jnp.concatenate([g * scale, jnp.zeros((HEAD_LANES - g.shape[0],), F32)])[None, :]


def _even_weights(j, ev_w_in, mla_q_norm_g, mla_w_uq, mla_kv_norm_g, mla_w_ukv, mla_qk_g_q, mla_qk_g_k):
    i_q = 3 * D_HY
    i_kv = i_q + MLA_Q_RANK
    i_kr = i_kv + MLA_KV_RANK
    w_in = ev_w_in[j]
    d = w_in.shape[0]
    kr_group = jnp.concatenate([jnp.zeros((d, MLA_NOPE), F32), w_in[:, i_kr:],
                                jnp.zeros((d, HEAD_LANES - MLA_NOPE - MLA_ROPE), F32)], axis=-1)
    w_in_p = jnp.concatenate([w_in[:, :i_kr], kr_group], axis=-1).astype(BF16)
    hq = MLA_NOPE + MLA_ROPE
    w_uq = mla_w_uq[j].reshape(MLA_Q_RANK, MLA_HEADS, hq)
    w_uq = jnp.pad(w_uq, ((0, 0), (0, 0), (0, HEAD_LANES - hq))).reshape(MLA_Q_RANK, -1).astype(BF16)
    w_ukv = mla_w_ukv[j].reshape(MLA_KV_RANK, MLA_HEADS, MLA_NOPE + MLA_V)
    w_uk = jnp.pad(w_ukv[:, :, :MLA_NOPE], ((0, 0), (0, 0), (0, HEAD_LANES - MLA_NOPE)))
    w_uk = w_uk.reshape(MLA_KV_RANK, -1).astype(BF16)
    w_uv = w_ukv[:, :, MLA_NOPE:].reshape(MLA_KV_RANK, -1).astype(BF16)
    return {
        "w_in": w_in_p, "q_norm_g": mla_q_norm_g[j][None, :], "kv_norm_g": mla_kv_norm_g[j][None, :],
        "w_uq": w_uq, "w_uk": w_uk, "w_uv": w_uv,
        "gq": _head_gain(mla_qk_g_q[j], MLA_SCALE * math.log2(math.e)), "gk": _head_gain(mla_qk_g_k[j], 1.0),
    }


def _rg_gate_weights(j, rg_w_a, rg_b_a, rg_w_x, rg_b_x, rg_lambda):
    wg = jnp.concatenate([rg_w_a[j], rg_w_x[j]], axis=-1)
    wg = jnp.transpose(wg, (1, 0, 2, 3)).astype(BF16)
    blk = lambda t: jnp.transpose(t.reshape(2, RG_BLOCKS, RG_BW), (1, 0, 2))
    bg = jnp.concatenate([blk(rg_b_a[j]), blk(rg_b_x[j])], axis=-1)[:, :, None, :]
    lam = blk(rg_lambda[j])[:, :, None, :]
    return wg, bg, lam


def kernel(x, c, ctx, c_ctx, ada_w, ada_b, norm1_g, norm2_g, ffn_w_in, ffn_w_out, ev_w_in, hy_short_w, hy_short_b, hy_w1, hy_b1, hy_w2, hy_b2, hy_w3, hy_freq, hy_bias, mla_q_norm_g, mla_w_uq, mla_kv_norm_g, mla_w_ukv, mla_qk_g_q, mla_qk_g_k, ev_w_out, rg_w_in, rg_conv_w, rg_conv_b, rg_w_a, rg_b_a, rg_w_x, rg_b_x, rg_lambda, rg_w_out):
    B, n_lat, D = x.shape
    n_ctx = ctx.shape[1]
    T = n_lat + n_ctx
    bm = 768
    assert T % bm == 0 and n_lat % 512 == 0 and n_ctx == 256

    cond8 = jnp.concatenate([c, c_ctx[None, :], jnp.zeros((8 - B - 1, D), F32)], axis=0)
    mods = ada_modulation(cond8, ada_w, ada_b)
    tabs = _rope_tables(n_lat, n_ctx)
    hy_tabs = _hy_tables(n_lat, n_ctx)
    x_all = jnp.concatenate([x, ctx], axis=1)

    for i in range(DEPTH):
        j = i // 2
        last = i == DEPTH - 1
        m = mods[i]
        g1 = norm1_g[i][None, :]
        g2 = norm2_g[i][None, :]
        if i % 2 == 0:
            w = _even_weights(j, ev_w_in, mla_q_norm_g, mla_w_uq, mla_kv_norm_g, mla_w_ukv, mla_qk_g_q, mla_qk_g_k)
            u_hy, q, k, v = even_pre(x_all, m, g1, w, tabs, n_lat, bm)
            att = mla_attend(q, k, v, n_lat)
            filt = (hy_w1[j], hy_b1[j], hy_w2[j], hy_b2[j], hy_w3[j], hy_freq[j])
            hy = hyena(u_hy, hy_short_w[j], hy_short_b[j][None, :], filt, hy_bias[j], hy_tabs, n_lat, n_ctx)
            w_out = ev_w_out[j].astype(BF16)
            mixes = [(hy, w_out[:D_HY]), (att, w_out[D_HY:])]
        else:
            gate, xr = odd_pre(x_all, m, g1, rg_w_in[j].astype(BF16), n_lat, bm)
            wg, bg, lam = _rg_gate_weights(j, rg_w_a, rg_b_a, rg_w_x, rg_b_x, rg_lambda)
            y = rglru(gate, xr, rg_conv_w[j], rg_conv_b[j][None, :], wg, bg, lam, n_lat)
            mixes = [(y, rg_w_out[j].astype(BF16))]
        wg_f = ffn_w_in[i][:, :D_FF].astype(BF16)
        wu_f = ffn_w_in[i][:, D_FF:].astype(BF16)
        wd_f = ffn_w_out[i].astype(BF16)
        if last:
            x_all = post_mixer(x_all, m, g2, mixes, wg_f, wu_f, wd_f, n_lat, n_lat, 512)
        else:
            x_all = post_mixer(x_all, m, g2, mixes, wg_f, wu_f, wd_f, n_lat, T, bm)
    return x_all
```

```python
import functools
import math

import jax
import jax.numpy as jnp
from jax import lax
from jax.experimental import pallas as pl
from jax.experimental.pallas import tpu as pltpu

F32 = jnp.float32
BF16 = jnp.bfloat16
U32 = jnp.uint32

D_MODEL = 1024
DEPTH = 4
GRID_W = 64
D_FF = 2816
NORM_EPS = 1e-6

D_HY = 512
HY_CB = 128
HY_NCB = D_HY // HY_CB
HY_N = 128
HY_PITCH = HY_N + 8
HY_BANDS = 16
HY_FAST_DECAY = 0.3
HY_SLOW_DECAY = 1.5
HY_TARGET = 1e-2

MLA_HEADS = 8
MLA_NOPE = 64
MLA_ROPE = 32
MLA_V = 64
MLA_VA = MLA_V + 16
MLA_Q_RANK = 384
MLA_KV_RANK = 256
MLA_SCALE = (MLA_NOPE + MLA_ROPE) ** -0.5
ROPE_AXIS = 16
ROPE_BASE = 10000.0
HEAD_LANES = 128

D_RNN = 1280
RG_BLOCKS = 10
RG_BW = 128
RG_CONV = 4
RG_C = 8.0

EVEN_IN_PAD = 3 * D_HY + MLA_Q_RANK + MLA_KV_RANK + HEAD_LANES

VMEM_LIMIT = 56 * 1024 * 1024


def _cparams(*sem):
    return pltpu.CompilerParams(dimension_semantics=sem, vmem_limit_bytes=VMEM_LIMIT)


def _const_spec(shape, single_buffer=False):
    nd = len(shape)
    kw = {"pipeline_mode": pl.Buffered(1)} if single_buffer else {}
    return pl.BlockSpec(shape, lambda *_: (0,) * nd, **kw)


def _ada_kernel(s_ref, w_ref, b_ref, o_ref):
    s = s_ref[...]
    s = s * jax.nn.sigmoid(s)
    o_ref[...] = jnp.dot(s, w_ref[...], preferred_element_type=F32) + b_ref[...]


def ada_modulation(cond8, ada_w, ada_b):
    depth, d, n = ada_w.shape
    tn = 1536
    return pl.pallas_call(
        _ada_kernel,
        grid=(depth, n // tn),
        in_specs=[
            pl.BlockSpec((8, d), lambda i, j: (0, 0)),
            pl.BlockSpec((None, d, tn), lambda i, j: (i, 0, j)),
            pl.BlockSpec((None, 1, tn), lambda i, j: (i, 0, j)),
        ],
        out_specs=pl.BlockSpec((None, 8, tn), lambda i, j: (i, 0, j)),
        out_shape=jax.ShapeDtypeStruct((depth, 8, n), F32),
        compiler_params=_cparams("parallel", "parallel"),
        name="ada_modulation",
    )(cond8, ada_w, ada_b.reshape(depth, 1, n))


def _mod_rows(m_ref, idx, b, is_ctx):
    d = D_MODEL
    lat = m_ref[pl.ds(b, 1), pl.ds(idx * d, d)]
    ctx = m_ref[pl.ds(2, 1), pl.ds(idx * d, d)]
    return jnp.where(is_ctx, ctx, lat)


def _rms(x, g):
    ms = jnp.mean(x * x, axis=-1, keepdims=True)
    return x * lax.rsqrt(ms + NORM_EPS) * g


def _ctx_rows(i, bm, n_lat):
    row = i * bm + lax.broadcasted_iota(jnp.int32, (bm, 1), 0)
    return row >= n_lat


def _even_pre_kernel(n_lat, x_ref, m_ref, g1_ref, w_in_ref, qg_ref, kvg_ref, w_uq_ref, w_uk_ref,
                     w_uv_ref, gq_ref, gk_ref, msq_ref, ct_ref, s1_ref, s2_ref,
                     hy_ref, q_ref, k_ref, v_ref):
    b = pl.program_id(0)
    i = pl.program_id(1)
    bm = x_ref.shape[0]
    is_ctx = _ctx_rows(i, bm, n_lat)
    x = x_ref[...]
    h = _rms(x, g1_ref[...])
    h = h * (1.0 + _mod_rows(m_ref, 1, b, is_ctx)) + _mod_rows(m_ref, 0, b, is_ctx)
    u = jnp.dot(h.astype(BF16), w_in_ref[...], preferred_element_type=F32)
    i_q = 3 * D_HY
    i_kv = i_q + MLA_Q_RANK
    i_kr = i_kv + MLA_KV_RANK
    hy_ref[...] = u[:, :i_q].astype(BF16)

    ct = ct_ref[...]
    s1 = s1_ref[...]
    s2 = s2_ref[...]
    msq = msq_ref[...]

    def head_norm(t, g):
        ms = jnp.dot((t * t).astype(BF16), msq, preferred_element_type=F32)
        return t * lax.rsqrt(ms + NORM_EPS) * g

    def rope(t):
        return (t * ct + pltpu.roll(t, HEAD_LANES - ROPE_AXIS // 2, 1) * s1
                + pltpu.roll(t, ROPE_AXIS // 2, 1) * s2)

    qn = _rms(u[:, i_q:i_kv], qg_ref[...]).astype(BF16)
    kvn = _rms(u[:, i_kv:i_kr], kvg_ref[...]).astype(BF16)
    gq = gq_ref[...]
    gk = gk_ref[...]
    k_rope = rope(head_norm(u[:, i_kr:], gk))
    for hd in range(MLA_HEADS):
        sl = slice(hd * HEAD_LANES, (hd + 1) * HEAD_LANES)
        qh = jnp.dot(qn, w_uq_ref[:, sl], preferred_element_type=F32)
        q_ref[hd] = rope(head_norm(qh, gq)).T.astype(BF16)
        kh = jnp.dot(kvn, w_uk_ref[:, sl], preferred_element_type=F32)
        k_ref[hd] = (head_norm(kh, gk) + k_rope).astype(BF16)
    v_t = jnp.dot(kvn, w_uv_ref[...], preferred_element_type=F32).T
    ones = jnp.ones((MLA_VA - MLA_V, bm), F32)
    for hd in range(MLA_HEADS):
        v_ref[hd] = jnp.concatenate([v_t[hd * MLA_V:(hd + 1) * MLA_V, :], ones], axis=0).astype(BF16)


def even_pre(x_all, m, g1, w, tabs, n_lat, bm):
    B, T, D = x_all.shape
    H = MLA_HEADS
    row = lambda n: pl.BlockSpec((None, bm, n), lambda b, i: (b, i, 0))
    tab = pl.BlockSpec((bm, HEAD_LANES), lambda b, i: (i, 0))
    headed = lambda nh: pl.BlockSpec((None, nh, bm, HEAD_LANES), lambda b, i: (b, 0, i, 0))
    consts = [m, g1, w["w_in"], w["q_norm_g"], w["kv_norm_g"], w["w_uq"], w["w_uk"], w["w_uv"],
              w["gq"], w["gk"], tabs["msq"]]
    return pl.pallas_call(
        functools.partial(_even_pre_kernel, n_lat),
        grid=(B, T // bm),
        in_specs=[row(D)] + [_const_spec(c.shape) for c in consts] + [tab, tab, tab],
        out_specs=[
            row(3 * D_HY),
            pl.BlockSpec((None, H, HEAD_LANES, bm), lambda b, i: (b, 0, 0, i)),
            headed(H),
            pl.BlockSpec((None, H, None, MLA_VA, bm), lambda b, i: (b, 0, i, 0, 0)),
        ],
        out_shape=[
            jax.ShapeDtypeStruct((B, T, 3 * D_HY), BF16),
            jax.ShapeDtypeStruct((B, H, HEAD_LANES, T), BF16),
            jax.ShapeDtypeStruct((B, H, T, HEAD_LANES), BF16),
            jax.ShapeDtypeStruct((B, H, T // bm, MLA_VA, bm), BF16),
        ],
        compiler_params=_cparams("parallel", "parallel"),
        name="even_pre",
    )(x_all, *consts, tabs["ct"], tabs["s1"], tabs["s2"])


def _attn_kernel(k_ref, qt_ref, vt_ref, o_ref, s_ref, mx_ref):
    n_chunks = k_ref.shape[1]
    tq = qt_ref.shape[2]
    assert n_chunks % 2 == 1

    def scores(c, slot):
        for hh in range(2):
            s = jnp.dot(k_ref[hh, c], qt_ref[hh], preferred_element_type=F32)
            s_ref[slot, hh] = s
            mx_ref[slot, hh] = jnp.max(s, axis=0, keepdims=True)

    def absorb(c, slot, stats):
        out = []
        for hh in range(2):
            m, acc = stats[hh]
            m_new = jnp.maximum(m, mx_ref[slot, hh])
            p = jnp.exp2(s_ref[slot, hh] - m_new)
            acc = jnp.exp2(m - m_new) * acc + jnp.dot(vt_ref[hh, c], p.astype(BF16), preferred_element_type=F32)
            out.append((m_new, acc))
        return tuple(out)

    def body(i, stats):
        c = 2 * i
        scores(c + 1, 1)
        stats = absorb(c, 0, stats)
        scores(c + 2, 0)
        return absorb(c + 1, 1, stats)

    init = tuple((jnp.full((1, tq), -jnp.inf, F32), jnp.zeros((MLA_VA, tq), F32)) for _ in range(2))
    scores(0, 0)
    stats = lax.fori_loop(0, n_chunks // 2, body, init)
    res = absorb(n_chunks - 1, 0, stats)
    out_t = jnp.concatenate([acc[:MLA_V] / acc[MLA_V:MLA_V + 1] for _, acc in res], axis=0)
    o_ref[...] = out_t.T.astype(o_ref.dtype)


def attention(k_chunks, q_t, v_t, q_blk0, n_q, tq):
    B, H, n_chunks, tk, _ = k_chunks.shape
    return pl.pallas_call(
        _attn_kernel,
        grid=(B, H // 2, n_q // tq),
        in_specs=[
            pl.BlockSpec((None, 2, n_chunks, tk, HEAD_LANES), lambda b, h, i: (b, h, 0, 0, 0)),
            pl.BlockSpec((None, 2, HEAD_LANES, tq), lambda b, h, i: (b, h, 0, q_blk0 + i)),
            pl.BlockSpec((None, 2, n_chunks, MLA_VA, tk), lambda b, h, i: (b, h, 0, 0, 0)),
        ],
        out_specs=pl.BlockSpec((None, tq, 2 * MLA_V), lambda b, h, i: (b, i, h)),
        out_shape=jax.ShapeDtypeStruct((B, n_q, H * MLA_V), BF16),
        scratch_shapes=[pltpu.VMEM((2, 2, tk, tq), F32), pltpu.VMEM((2, 2, 1, tq), F32)],
        compiler_params=_cparams("parallel", "parallel", "parallel"),
        name="mla_attention",
    )(k_chunks, q_t, v_t)


def mla_attend(q_t, k, v_t, n_lat, tq=1024):
    B, H, T, _ = k.shape
    n_chunks, tk = v_t.shape[2], v_t.shape[4]
    n_ctx = T - n_lat
    att_l = attention(k.reshape(B, H, n_chunks, tk, HEAD_LANES), q_t, v_t, 0, n_lat, tq)
    k_c = k[:, :, n_lat:].reshape(B, H, 1, n_ctx, HEAD_LANES)
    v_c = v_t[:, :, n_chunks - 1, :, tk - n_ctx:].reshape(B, H, 1, MLA_VA, n_ctx)
    att_c = attention(k_c, q_t, v_c, n_lat // n_ctx, n_ctx, n_ctx)
    return jnp.concatenate([att_l, att_c], axis=1)


def _post_kernel(n_mix, n_lat, tf, *refs):
    x_ref, m_ref, g2_ref = refs[:3]
    mix = refs[3:3 + 2 * n_mix]
    wg_ref, wu_ref, wd_ref, o_ref = refs[3 + 2 * n_mix:]
    b = pl.program_id(0)
    i = pl.program_id(1)
    bm = x_ref.shape[0]
    is_ctx = _ctx_rows(i, bm, n_lat)
    y = None
    for j in range(n_mix):
        t = jnp.dot(mix[2 * j][...], mix[2 * j + 1][...], preferred_element_type=F32)
        y = t if y is None else y + t
    x1 = x_ref[...] + _mod_rows(m_ref, 2, b, is_ctx) * y
    h = _rms(x1, g2_ref[...])
    h = (h * (1.0 + _mod_rows(m_ref, 4, b, is_ctx)) + _mod_rows(m_ref, 3, b, is_ctx)).astype(BF16)
    acc = None
    for f0 in range(0, D_FF, tf):
        g = jnp.dot(h, wg_ref[:, f0:f0 + tf], preferred_element_type=F32)
        u = jnp.dot(h, wu_ref[:, f0:f0 + tf], preferred_element_type=F32)
        a = (g * jax.nn.sigmoid(g) * u).astype(BF16)
        t = jnp.dot(a, wd_ref[f0:f0 + tf, :], preferred_element_type=F32)
        acc = t if acc is None else acc + t
    o_ref[...] = x1 + _mod_rows(m_ref, 5, b, is_ctx) * acc


def post_mixer(x_all, m, g2, mixes, wg, wu, wd, n_lat, n_rows, bm, tf=1408):
    B, T, D = x_all.shape
    row = lambda n: pl.BlockSpec((None, bm, n), lambda b, i: (b, i, 0))
    mix_args, mix_specs = [], []
    for a, wmat in mixes:
        mix_args += [a, wmat]
        mix_specs += [row(a.shape[-1]), _const_spec(wmat.shape, True)]
    return pl.pallas_call(
        functools.partial(_post_kernel, len(mixes), n_lat, tf),
        grid=(B, n_rows // bm),
        in_specs=[row(D), _const_spec(m.shape), _const_spec(g2.shape)] + mix_specs
        + [_const_spec(wg.shape, True), _const_spec(wu.shape, True), _const_spec(wd.shape, True)],
        out_specs=row(D),
        out_shape=jax.ShapeDtypeStruct((B, n_rows, D), F32),
        compiler_params=_cparams("parallel", "parallel"),
        name="post_mixer",
    )(x_all, m, g2, *mix_args, wg, wu, wd)


def _odd_pre_kernel(n_lat, x_ref, m_ref, g1_ref, w_ref, gate_ref, xr_ref):
    b = pl.program_id(0)
    i = pl.program_id(1)
    bm = x_ref.shape[0]
    is_ctx = _ctx_rows(i, bm, n_lat)
    h = _rms(x_ref[...], g1_ref[...])
    h = h * (1.0 + _mod_rows(m_ref, 1, b, is_ctx)) + _mod_rows(m_ref, 0, b, is_ctx)
    u = jnp.dot(h.astype(BF16), w_ref[...], preferred_element_type=F32)
    gate_ref[...] = u[:, :D_RNN].astype(BF16)
    xr_ref[...] = u[:, D_RNN:]


def odd_pre(x_all, m, g1, w_in, n_lat, bm):
    B, T, D = x_all.shape
    row = lambda n: pl.BlockSpec((None, bm, n), lambda b, i: (b, i, 0))
    return pl.pallas_call(
        functools.partial(_odd_pre_kernel, n_lat),
        grid=(B, T // bm),
        in_specs=[row(D), _const_spec(m.shape), _const_spec(g1.shape), _const_spec(w_in.shape)],
        out_specs=[row(D_RNN), row(D_RNN)],
        out_shape=[jax.ShapeDtypeStruct((B, T, D_RNN), BF16), jax.ShapeDtypeStruct((B, T, D_RNN), F32)],
        compiler_params=_cparams("parallel", "parallel"),
        name="odd_pre",
    )(x_all, m, g1, w_in)


def _rglru_kernel(n_lat, chunk, gate_ref, xr_ref, cw_ref, cb_ref, wg_ref, bg_ref, lam_ref, o_ref, hf_ref, xc_ref):
    T = xr_ref.shape[0]
    n_ctx = T - n_lat
    R = chunk
    nv = R // 8
    cw = cw_ref[...]
    cb = cb_ref[...]
    sub = lax.broadcasted_iota(jnp.int32, (nv, 8, RG_BW), 1)

    def conv_chunk(c0, seg_lo, seg_hi):
        lo = jnp.maximum(c0 - 8, 0)
        hi = jnp.minimum(c0 + R, T - 8)
        win = jnp.concatenate([xr_ref[pl.ds(pl.multiple_of(lo, 8), 8), :],
                               xr_ref[pl.ds(pl.multiple_of(c0, 8), R), :],
                               xr_ref[pl.ds(pl.multiple_of(hi, 8), 8), :]], axis=0)
        g = c0 - 8 + lax.broadcasted_iota(jnp.int32, (R + 16, 1), 0)
        win = jnp.where((g >= seg_lo) & (g < seg_hi), win, 0.0)
        y = cb
        for kk in range(RG_CONV):
            off = 8 + kk - RG_CONV // 2
            y = y + win[off:off + R, :] * cw[kk:kk + 1, :]
        return y

    def gates(xc, d):
        pre = jnp.dot(xc.astype(BF16), wg_ref[d], preferred_element_type=F32) + bg_ref[d]
        r = 0.5 * jnp.tanh(0.5 * pre[:, :RG_BW]) + 0.5
        ig = 0.5 * jnp.tanh(0.5 * pre[:, RG_BW:]) + 0.5
        log_a = r * (-RG_C * jax.nn.softplus(-lam_ref[d]))
        a = jnp.exp(log_a)
        bt = jnp.sqrt(-jnp.tanh(log_a) * (a * a + 1.0)) * (ig * xc)
        return a, bt

    def local_scan(a, bt, reverse):
        a = a.reshape(nv, 8, RG_BW)
        bt = bt.reshape(nv, 8, RG_BW)
        for d in (1, 2, 4):
            if reverse:
                keep = sub < 8 - d
                sh = 8 - d
            else:
                keep = sub >= d
                sh = d
            a_s = jnp.where(keep, pltpu.roll(a, sh, 1), 1.0)
            b_s = jnp.where(keep, pltpu.roll(bt, sh, 1), 0.0)
            bt = a * b_s + bt
            a = a * a_s
        return a, bt

    def chunk_scan(a, bt, h, reverse):
        a, bt = local_scan(a, bt, reverse)
        rows = [None] * nv
        order = range(nv - 1, -1, -1) if reverse else range(nv)
        edge = 0 if reverse else 7
        for v in order:
            hv = bt[v] + a[v] * h
            rows[v] = hv
            h = hv[edge:edge + 1, :]
        return jnp.concatenate(rows, axis=0), h

    n_lat_chunks = n_lat // R
    h0 = jnp.zeros((1, RG_BW), F32)

    def fwd_chunk(c0, seg_lo, seg_hi, h):
        xc = conv_chunk(c0, seg_lo, seg_hi)
        a, bt = gates(xc, 0)
        hs, h = chunk_scan(a, bt, h, False)
        rs = pl.ds(pl.multiple_of(c0, 8), R)
        hf_ref[rs, :] = hs
        xc_ref[rs, :] = xc
        return h

    h = h0
    for c in range(n_ctx // R):
        h = fwd_chunk(n_lat + c * R, n_lat, T, h)
    h = lax.fori_loop(0, n_lat_chunks, lambda c, hh: fwd_chunk(c * R, 0, n_lat, hh), h)

    def bwd_chunk(c0, h):
        rs = pl.ds(pl.multiple_of(c0, 8), R)
        a, bt = gates(xc_ref[rs, :], 1)
        hs, h = chunk_scan(a, bt, h, True)
        g = gate_ref[rs, :].astype(F32)
        o_ref[rs, :] = (jax.nn.gelu(g) * (hf_ref[rs, :] + hs)).astype(o_ref.dtype)
        return h

    h = h0
    for c in range(n_ctx // R - 1, -1, -1):
        h = bwd_chunk(n_lat + c * R, h)
    lax.fori_loop(0, n_lat_chunks, lambda c, hh: bwd_chunk((n_lat_chunks - 1 - c) * R, hh), h)


def rglru(gate, xr, cw, cb, wg, bg, lam, n_lat, chunk=256):
    B, T, _ = xr.shape
    col = pl.BlockSpec((None, T, RG_BW), lambda b, n: (b, 0, n))
    return pl.pallas_call(
        functools.partial(_rglru_kernel, n_lat, chunk),
        grid=(B, RG_BLOCKS),
        in_specs=[
            col, col,
            pl.BlockSpec((RG_CONV, RG_BW), lambda b, n: (0, n)),
            pl.BlockSpec((1, RG_BW), lambda b, n: (0, n)),
            pl.BlockSpec((None, 2, RG_BW, 2 * RG_BW), lambda b, n: (n, 0, 0, 0)),
            pl.BlockSpec((None, 2, 1, 2 * RG_BW), lambda b, n: (n, 0, 0, 0)),
            pl.BlockSpec((None, 2, 1, RG_BW), lambda b, n: (n, 0, 0, 0)),
        ],
        out_specs=col,
        out_shape=jax.ShapeDtypeStruct((B, T, D_RNN), BF16),
        scratch_shapes=[pltpu.VMEM((T, RG_BW), F32), pltpu.VMEM((T, RG_BW), F32)],
        compiler_params=_cparams("parallel", "parallel"),
        name="rglru",
    )(gate, xr, cw, cb, wg, bg, lam)


def _hyena_filters(L, w1, b1, w2, b2, w3, freq):
    k = jnp.arange(L, dtype=F32)
    t = (k / max(L - 1, 1))[:, None]
    ang = (2.0 * math.pi / L) * k[:, None] * jnp.linspace(1e-4, HY_BANDS - 1, HY_BANDS, dtype=F32)[None, :]
    z = jnp.concatenate([t, jnp.cos(ang), -jnp.sin(ang)], axis=-1)
    mm = functools.partial(jnp.dot, precision=lax.Precision.HIGHEST)
    h = jnp.sin(freq * (mm(z, w1) + b1))
    h = jnp.sin(freq * (mm(h, w2) + b2))
    h = mm(h, w3).reshape(L, 2, D_HY)
    half = L // 2
    dist = jnp.abs(k - half) / max(half, 1)
    deltas = jnp.abs(jnp.linspace(math.log(HY_TARGET) / HY_SLOW_DECAY, math.log(HY_TARGET) / HY_FAST_DECAY,
                                  D_HY, dtype=F32))
    window = jnp.exp(-dist[:, None] * deltas[None, :])
    return jnp.transpose(h * window[:, None, :], (1, 0, 2))


def _swap_neg(x):
    return jnp.concatenate([x[:, HY_CB:], -x[:, :HY_CB]], axis=1)


def _cstack(x):
    return jnp.concatenate([x, _swap_neg(x)], axis=0)


def _cmul(h, w):
    hre = jnp.concatenate([h[:, :HY_CB], h[:, :HY_CB]], axis=1)
    him = jnp.concatenate([-h[:, HY_CB:], h[:, HY_CB:]], axis=1)
    wsw = jnp.concatenate([w[:, HY_CB:], w[:, :HY_CB]], axis=1)
    return hre * w + him * wsw


def _hy_short_kernel(u_ref, w_ref, b_ref, o_ref):
    L = u_ref.shape[1]
    row = lax.broadcasted_iota(jnp.int32, (L, 1), 0)
    w = w_ref[...]
    halves = []
    for b in range(2):
        u = u_ref[b].astype(F32)
        prev = jnp.where(row >= 1, pltpu.roll(u, 1, 0), 0.0)
        nxt = jnp.where(row < L - 1, pltpu.roll(u, L - 1, 0), 0.0)
        halves.append(prev * w[0:1] + u * w[1:2] + nxt * w[2:3] + b_ref[...])
    o_ref[...] = jnp.concatenate(halves, axis=1).astype(o_ref.dtype)


def hy_short(u_hy, short_w, short_b, row_blk, L):
    return pl.pallas_call(
        _hy_short_kernel,
        grid=(3, HY_NCB),
        in_specs=[
            pl.BlockSpec((2, L, HY_CB), lambda g, cb: (0, row_blk, g * HY_NCB + cb)),
            pl.BlockSpec((3, HY_CB), lambda g, cb: (0, g * HY_NCB + cb)),
            pl.BlockSpec((1, HY_CB), lambda g, cb: (0, g * HY_NCB + cb)),
        ],
        out_specs=pl.BlockSpec((None, None, L, 2 * HY_CB), lambda g, cb: (g, cb, 0, 0)),
        out_shape=jax.ShapeDtypeStruct((3, HY_NCB, L, 2 * HY_CB), BF16),
        compiler_params=_cparams("parallel", "parallel"),
        name="hy_short",
    )(u_hy, short_w, short_b)


def _hy_conv_kernel(with_filter, z_ref, ma_ref, fb_ref, *rest):
    if with_filter:
        h_ref, fc_ref, md_ref, o_ref, work = rest
    else:
        o_ref, work = rest
    n1 = z_ref.shape[0] // HY_N
    unroll = 8

    def put(rows, val):
        work[0, rows, :] = val[:, :HY_CB]
        work[1, rows, :] = val[:, HY_CB:]

    def get(rows):
        return jnp.concatenate([work[0, rows, :], work[1, rows, :]], axis=1).astype(BF16)

    def stage_a(i, carry):
        for g in range(unroll):
            j2 = i * unroll + g
            x = z_ref[pl.ds(pl.multiple_of(j2 * n1, 16), n1), :]
            a = jnp.dot(ma_ref[j2], _cstack(x), preferred_element_type=F32)
            put(pl.ds(j2, HY_N, stride=HY_PITCH), a)
        return carry

    lax.fori_loop(0, HY_N // unroll, stage_a, 0)

    def mid(i, carry):
        for g in range(unroll):
            k1 = i * unroll + g
            slab = pl.ds(pl.multiple_of(k1 * HY_PITCH, 8), HY_N)
            rows = pl.ds(pl.multiple_of(k1 * HY_N, 16), HY_N)
            w = jnp.dot(fb_ref[...], _cstack(get(slab)), preferred_element_type=F32)
            if with_filter:
                y = _cmul(h_ref[rows, :].astype(F32), w)
                put(slab, jnp.dot(fc_ref[...], _cstack(y.astype(BF16)), preferred_element_type=F32))
            else:
                o_ref[rows, :] = w.astype(o_ref.dtype)
        return carry

    lax.fori_loop(0, HY_N // unroll, mid, 0)
    if not with_filter:
        return

    def stage_d(i, carry):
        for g in range(unroll):
            q2 = i * unroll + g
            y = jnp.dot(md_ref[q2], _cstack(get(pl.ds(q2, HY_N, stride=HY_PITCH))), preferred_element_type=F32)
            o_ref[pl.ds(pl.multiple_of(q2 * n1, 16), n1), :] = y.astype(o_ref.dtype)
        return carry

    lax.fori_loop(0, HY_N // unroll, stage_d, 0)


def hy_conv(z_t, tabs, spectrum=None):
    ncb, L, lanes = z_t.shape
    once = {"pipeline_mode": pl.Buffered(1)}
    blk = lambda rows: pl.BlockSpec((None, rows, lanes), lambda cb: (cb, 0, 0), **once)
    args = [z_t, tabs["ma"], tabs["fb"]]
    specs = [blk(L), _const_spec(tabs["ma"].shape, True), _const_spec(tabs["fb"].shape, True)]
    n_out = HY_N * HY_N
    if spectrum is not None:
        args += [spectrum, tabs["fc"], tabs["md"]]
        specs += [blk(HY_N * HY_N), _const_spec(tabs["fc"].shape, True), _const_spec(tabs["md"].shape, True)]
        n_out = L
    return pl.pallas_call(
        functools.partial(_hy_conv_kernel, spectrum is not None),
        grid=(ncb,),
        in_specs=specs,
        out_specs=pl.BlockSpec((None, n_out, lanes), lambda cb: (cb, 0, 0)),
        out_shape=jax.ShapeDtypeStruct((ncb, n_out, lanes), BF16),
        scratch_shapes=[pltpu.VMEM((2, HY_N * HY_PITCH, HY_CB), F32)],
        compiler_params=_cparams("parallel"),
        name="hy_conv",
    )(*args)


def _hy_gate_kernel(unpack, x_ref, y_ref, z_ref, b_ref, o_ref):
    z = x_ref[...].astype(F32) * (y_ref[...].astype(F32) + b_ref[...] * z_ref[...].astype(F32))
    if unpack:
        o_ref[0] = z[:, :HY_CB].astype(o_ref.dtype)
        o_ref[1] = z[:, HY_CB:].astype(o_ref.dtype)
    else:
        o_ref[...] = z.astype(o_ref.dtype)


def hy_gate(x, gx, y, z, gz, bias_p, unpack, rows=2048):
    ncb, L, lanes = y.shape
    blk = pl.BlockSpec((None, rows, lanes), lambda cb, r: (cb, r, 0))
    pick = lambda g: pl.BlockSpec((None, None, rows, lanes), lambda cb, r: (g, cb, r, 0))
    if unpack:
        out_spec = pl.BlockSpec((2, rows, HY_CB), lambda cb, r: (0, r, cb))
        out_shape = jax.ShapeDtypeStruct((2, L, ncb * HY_CB), BF16)
    else:
        out_spec, out_shape = blk, jax.ShapeDtypeStruct(y.shape, BF16)
    return pl.pallas_call(
        functools.partial(_hy_gate_kernel, unpack),
        grid=(ncb, L // rows),
        in_specs=[pick(gx), blk, pick(gz), pl.BlockSpec((None, 1, lanes), lambda cb, r: (cb, 0, 0))],
        out_specs=out_spec,
        out_shape=out_shape,
        compiler_params=_cparams("parallel", "parallel"),
        name="hy_gate",
    )(x, y, z, bias_p)


def _hy_ctx_kernel(s_ref, h_ref, b_ref, ff_ref, fi_ref, o_ref):
    z = s_ref[0].astype(F32)
    L = z.shape[0]
    ff = ff_ref[...]
    for o in range(2):
        hp = jnp.concatenate([h_ref[o], jnp.zeros((L, HY_CB), F32)], axis=1)
        spec = jnp.dot(ff, _cstack(hp.astype(BF16)), preferred_element_type=F32)
        w = jnp.dot(ff, _cstack(z.astype(BF16)), preferred_element_type=F32)
        y = jnp.dot(fi_ref[...], _cstack(_cmul(spec, w).astype(BF16)), preferred_element_type=F32)
        z = s_ref[o + 1].astype(F32) * (y + b_ref[o] * z)
    o_ref[0] = z[:, :HY_CB].astype(o_ref.dtype)
    o_ref[1] = z[:, HY_CB:].astype(o_ref.dtype)


def hy_ctx(s, h, bias_p, ff, fi):
    _, ncb, L, _ = s.shape
    return pl.pallas_call(
        _hy_ctx_kernel,
        grid=(ncb,),
        in_specs=[
            pl.BlockSpec((3, None, L, 2 * HY_CB), lambda cb: (0, cb, 0, 0)),
            pl.BlockSpec((2, L, HY_CB), lambda cb: (0, 0, cb)),
            pl.BlockSpec((2, None, 1, 2 * HY_CB), lambda cb: (0, cb, 0, 0)),
            _const_spec(ff.shape), _const_spec(fi.shape),
        ],
        out_specs=pl.BlockSpec((2, L, HY_CB), lambda cb: (0, 0, cb)),
        out_shape=jax.ShapeDtypeStruct((2, L, ncb * HY_CB), BF16),
        compiler_params=_cparams("parallel"),
        name="hy_ctx",
    )(s, h, bias_p, ff, fi)


def _hy_tables(L, L_ctx):
    n = 2 * L
    n1 = L // HY_N
    two_pi = 2.0 * math.pi
    ar = lambda m: jnp.arange(m, dtype=jnp.int32)
    cs = lambda m, period: (jnp.cos(m.astype(F32) * (two_pi / period)), jnp.sin(m.astype(F32) * (two_pi / period)))
    tc, ts = cs((ar(HY_N)[:, None] * ar(HY_N)[None, :]) % n, n)
    c1, s1 = cs((ar(HY_N)[:, None] * ar(n1)[None, :]) % HY_N, HY_N)
    c = c1[None] * tc[:, :, None] - s1[None] * ts[:, :, None]
    s = s1[None] * tc[:, :, None] + c1[None] * ts[:, :, None]
    ma = jnp.concatenate([c, s], axis=-1).astype(BF16)
    q1 = ar(n1) + n1 // 2
    c1, s1 = cs((q1[:, None] * ar(HY_N)[None, :]) % HY_N, HY_N)
    c = c1[None] * tc[:, None, :] - s1[None] * ts[:, None, :]
    s = s1[None] * tc[:, None, :] + c1[None] * ts[:, None, :]
    md = (jnp.concatenate([c, -s], axis=-1) / n).astype(BF16)
    m = (ar(HY_N)[:, None] * ar(HY_N)[None, :]) % HY_N
    c, s = cs(m, HY_N)
    fb = jnp.concatenate([c, s], axis=-1).astype(BF16)
    fc = jnp.concatenate([c, -s], axis=-1).astype(BF16)
    nc = 2 * L_ctx
    m = (ar(nc)[:, None] * ar(L_ctx)[None, :]) % nc
    c, s = cs(m, nc)
    ff = jnp.concatenate([c, s], axis=-1).astype(BF16)
    m = ((ar(L_ctx) + L_ctx // 2)[:, None] * ar(nc)[None, :]) % nc
    c, s = cs(m, nc)
    fi = (jnp.concatenate([c, -s], axis=-1) / nc).astype(BF16)
    return {"ma": ma, "md": md, "fb": fb, "fc": fc, "ff": ff, "fi": fi}


def hyena(u_hy, short_w, short_b, filt, bias, tabs, n_lat, n_ctx):
    bias_p = jnp.concatenate([bias.reshape(2, HY_NCB, 1, HY_CB)] * 2, axis=-1)
    n1 = n_lat // HY_N
    to_slabs = lambda t: jnp.swapaxes(t.reshape(HY_NCB, n1, HY_N, -1), 1, 2).reshape(HY_NCB, n_lat, -1)
    to_time = lambda t: jnp.swapaxes(t.reshape(HY_NCB, HY_N, n1, -1), 1, 2).reshape(HY_NCB, n_lat, -1)
    h_lat = _hyena_filters(n_lat, *filt)
    h_lat = jnp.transpose(h_lat.reshape(2, n1, HY_N, HY_NCB, HY_CB), (0, 3, 2, 1, 4)).reshape(2, HY_NCB, n_lat, HY_CB)
    h_lat = jnp.concatenate([h_lat, jnp.zeros_like(h_lat)], axis=-1).astype(BF16)
    s = hy_short(u_hy, short_w, short_b, 0, n_lat)
    z, gz = s, 0
    for o in range(2):
        y = hy_conv(to_slabs(z[gz]), tabs, hy_conv(h_lat[o], tabs))
        z, gz = hy_gate(s, o + 1, to_time(y), z, gz, bias_p[o], o == 1)[None], 0
    s_c = hy_short(u_hy, short_w, short_b, n_lat // n_ctx, n_ctx)
    z_c = hy_ctx(s_c, _hyena_filters(n_ctx, *filt), bias_p, tabs["ff"], tabs["fi"])
    return jnp.concatenate([z[0], z_c], axis=1)


def _rope_tables(n_lat, n_ctx):
    rows = n_lat // GRID_W
    row = jnp.repeat(jnp.arange(rows, dtype=F32), GRID_W)
    col = jnp.tile(jnp.arange(GRID_W, dtype=F32), rows)
    inv_freq = ROPE_BASE ** (-jnp.arange(0, ROPE_AXIS, 2, dtype=F32) / ROPE_AXIS)
    ang = jnp.concatenate([row[:, None] * inv_freq[None, :], col[:, None] * inv_freq[None, :]], axis=-1)
    ang = jnp.concatenate([ang, jnp.zeros((n_ctx, ROPE_AXIS), F32)], axis=0)
    cos, sin = jnp.cos(ang), jnp.sin(ang)
    T = n_lat + n_ctx
    z8 = jnp.zeros((T, ROPE_AXIS // 2), F32)
    cos32 = jnp.concatenate([cos[:, :8], cos[:, :8], cos[:, 8:], cos[:, 8:]], axis=-1)
    first = jnp.concatenate([-sin[:, :8], z8, -sin[:, 8:], z8], axis=-1)
    second = jnp.concatenate([z8, sin[:, :8], z8, sin[:, 8:]], axis=-1)
    pad = lambda t, fill: jnp.concatenate(
        [jnp.full((T, MLA_NOPE), fill, F32), t, jnp.zeros((T, HEAD_LANES - MLA_NOPE - MLA_ROPE), F32)], axis=-1)
    lane = jnp.arange(HEAD_LANES)
    seg = jnp.where(lane < MLA_NOPE, 0, jnp.where(lane < MLA_NOPE + MLA_ROPE, 1, 2))
    msq = (seg[:, None] == seg[None, :]) & (seg[:, None] < 2)
    msq = jnp.where(msq, jnp.where(seg[:, None] == 0, 1.0 / MLA_NOPE, 1.0 / MLA_ROPE), 0.0).astype(BF16)
    return {"ct": pad(cos32, 1.0), "s1": pad(first, 0.0), "s2": pad(second, 0.0), "msq": msq}


def _head_gain(g, scale):
    return jnp.concatenate([g * scale, jnp.zeros((HEAD_LANES - g.shape[0],), F32)])[None, :]


def _even_weights(j, ev_w_in, mla_q_norm_g, mla_w_uq, mla_kv_norm_g, mla_w_ukv, mla_qk_g_q, mla_qk_g_k):
    i_q = 3 * D_HY
    i_kv = i_q + MLA_Q_RANK
    i_kr = i_kv + MLA_KV_RANK
    w_in = ev_w_in[j]
    d = w_in.shape[0]
    kr_group = jnp.concatenate([jnp.zeros((d, MLA_NOPE), F32), w_in[:, i_kr:],
                                jnp.zeros((d, HEAD_LANES - MLA_NOPE - MLA_ROPE), F32)], axis=-1)
    w_in_p = jnp.concatenate([w_in[:, :i_kr], kr_group], axis=-1).astype(BF16)
    hq = MLA_NOPE + MLA_ROPE
    w_uq = mla_w_uq[j].reshape(MLA_Q_RANK, MLA_HEADS, hq)
    w_uq = jnp.pad(w_uq, ((0, 0), (0, 0), (0, HEAD_LANES - hq))).reshape(MLA_Q_RANK, -1).astype(BF16)
    w_ukv = mla_w_ukv[j].reshape(MLA_KV_RANK, MLA_HEADS, MLA_NOPE + MLA_V)
    w_uk = jnp.pad(w_ukv[:, :, :MLA_NOPE], ((0, 0), (0, 0), (0, HEAD_LANES - MLA_NOPE)))
    w_uk = w_uk.reshape(MLA_KV_RANK, -1).astype(BF16)
    w_uv = w_ukv[:, :, MLA_NOPE:].reshape(MLA_KV_RANK, -1).astype(BF16)
    return {
        "w_in": w_in_p, "q_norm_g": mla_q_norm_g[j][None, :], "kv_norm_g": mla_kv_norm_g[j][None, :],
        "w_uq": w_uq, "w_uk": w_uk, "w_uv": w_uv,
        "gq": _head_gain(mla_qk_g_q[j], MLA_SCALE * math.log2(math.e)), "gk": _head_gain(mla_qk_g_k[j], 1.0),
    }


def _rg_gate_weights(j, rg_w_a, rg_b_a, rg_w_x, rg_b_x, rg_lambda):
    wg = jnp.concatenate([rg_w_a[j], rg_w_x[j]], axis=-1)
    wg = jnp.transpose(wg, (1, 0, 2, 3)).astype(BF16)
    blk = lambda t: jnp.transpose(t.reshape(2, RG_BLOCKS, RG_BW), (1, 0, 2))
    bg = jnp.concatenate([blk(rg_b_a[j]), blk(rg_b_x[j])], axis=-1)[:, :, None, :]
    lam = blk(rg_lambda[j])[:, :, None, :]
    return wg, bg, lam


def kernel(x, c, ctx, c_ctx, ada_w, ada_b, norm1_g, norm2_g, ffn_w_in, ffn_w_out, ev_w_in, hy_short_w, hy_short_b, hy_w1, hy_b1, hy_w2, hy_b2, hy_w3, hy_freq, hy_bias, mla_q_norm_g, mla_w_uq, mla_kv_norm_g, mla_w_ukv, mla_qk_g_q, mla_qk_g_k, ev_w_out, rg_w_in, rg_conv_w, rg_conv_b, rg_w_a, rg_b_a, rg_w_x, rg_b_x, rg_lambda, rg_w_out):
    B, n_lat, D = x.shape
    n_ctx = ctx.shape[1]
    T = n_lat + n_ctx
    bm = 768
    assert T % bm == 0 and n_lat % 512 == 0 and n_ctx == 256

    cond8 = jnp.concatenate([c, c_ctx[None, :], jnp.zeros((8 - B - 1, D), F32)], axis=0)
    mods = ada_modulation(cond8, ada_w, ada_b)
    tabs = _rope_tables(n_lat, n_ctx)
    hy_tabs = _hy_tables(n_lat, n_ctx)
    x_all = jnp.concatenate([x, ctx], axis=1)

    for i in range(DEPTH):
        j = i // 2
        last = i == DEPTH - 1
        m = mods[i]
        g1 = norm1_g[i][None, :]
        g2 = norm2_g[i][None, :]
        if i % 2 == 0:
            w = _even_weights(j, ev_w_in, mla_q_norm_g, mla_w_uq, mla_kv_norm_g, mla_w_ukv, mla_qk_g_q, mla_qk_g_k)
            u_hy, q, k, v = even_pre(x_all, m, g1, w, tabs, n_lat, bm)
            att = mla_attend(q, k, v, n_lat)
            filt = (hy_w1[j], hy_b1[j], hy_w2[j], hy_b2[j], hy_w3[j], hy_freq[j])
            hy = hyena(u_hy, hy_short_w[j], hy_short_b[j][None, :], filt, hy_bias[j], hy_tabs, n_lat, n_ctx)
            w_out = ev_w_out[j].astype(BF16)
            mixes = [(hy, w_out[:D_HY]), (att, w_out[D_HY:])]
        else:
            gate, xr = odd_pre(x_all, m, g1, rg_w_in[j].astype(BF16), n_lat, bm)
            wg, bg, lam = _rg_gate_weights(j, rg_w_a, rg_b_a, rg_w_x, rg_b_x, rg_lambda)
            y = rglru(gate, xr, rg_conv_w[j], rg_conv_b[j][None, :], wg, bg, lam, n_lat)
            mixes = [(y, rg_w_out[j].astype(BF16))]
        wg_f = ffn_w_in[i][:, :D_FF].astype(BF16)
        wu_f = ffn_w_in[i][:, D_FF:].astype(BF16)
        wd_f = ffn_w_out[i].astype(BF16)
        if last:
            x_all = post_mixer(x_all, m, g2, mixes, wg_f, wu_f, wd_f, n_lat, n_lat, 512)
        else:
            x_all = post_mixer(x_all, m, g2, mixes, wg_f, wu_f, wd_f, n_lat, T, bm)
    return x_all
```

```python
import functools
import math

import jax
import jax.numpy as jnp
from jax import lax
from jax.experimental import pallas as pl
from jax.experimental.pallas import tpu as pltpu

F32 = jnp.float32
BF16 = jnp.bfloat16
U32 = jnp.uint32

D_MODEL = 1024
DEPTH = 4
GRID_W = 64
D_FF = 2816
NORM_EPS = 1e-6

D_HY = 512
HY_CB = 128
HY_NCB = D_HY // HY_CB
HY_N = 128
HY_PITCH = HY_N + 8
HY_BANDS = 16
HY_FAST_DECAY = 0.3
HY_SLOW_DECAY = 1.5
HY_TARGET = 1e-2

MLA_HEADS = 8
MLA_NOPE = 64
MLA_ROPE = 32
MLA_V = 64
MLA_VA = MLA_V + 16
MLA_Q_RANK = 384
MLA_KV_RANK = 256
MLA_SCALE = (MLA_NOPE + MLA_ROPE) ** -0.5
ROPE_AXIS = 16
ROPE_BASE = 10000.0
HEAD_LANES = 128

D_RNN = 1280
RG_BLOCKS = 10
RG_BW = 128
RG_CONV = 4
RG_C = 8.0

EVEN_IN_PAD = 3 * D_HY + MLA_Q_RANK + MLA_KV_RANK + HEAD_LANES

VMEM_LIMIT = 56 * 1024 * 1024


def _cparams(*sem):
    return pltpu.CompilerParams(dimension_semantics=sem, vmem_limit_bytes=VMEM_LIMIT)


def _const_spec(shape, single_buffer=False):
    nd = len(shape)
    kw = {"pipeline_mode": pl.Buffered(1)} if single_buffer else {}
    return pl.BlockSpec(shape, lambda *_: (0,) * nd, **kw)


def _ada_kernel(s_ref, w_ref, b_ref, o_ref):
    s = s_ref[...]
    s = s * jax.nn.sigmoid(s)
    o_ref[...] = jnp.dot(s, w_ref[...], preferred_element_type=F32) + b_ref[...]


def ada_modulation(cond8, ada_w, ada_b):
    depth, d, n = ada_w.shape
    tn = 1536
    return pl.pallas_call(
        _ada_kernel,
        grid=(depth, n // tn),
        in_specs=[
            pl.BlockSpec((8, d), lambda i, j: (0, 0)),
            pl.BlockSpec((None, d, tn), lambda i, j: (i, 0, j)),
            pl.BlockSpec((None, 1, tn), lambda i, j: (i, 0, j)),
        ],
        out_specs=pl.BlockSpec((None, 8, tn), lambda i, j: (i, 0, j)),
        out_shape=jax.ShapeDtypeStruct((depth, 8, n), F32),
        compiler_params=_cparams("parallel", "parallel"),
        name="ada_modulation",
    )(cond8, ada_w, ada_b.reshape(depth, 1, n))


def _mod_rows(m_ref, idx, b, is_ctx):
    d = D_MODEL
    lat = m_ref[pl.ds(b, 1), pl.ds(idx * d, d)]
    ctx = m_ref[pl.ds(2, 1), pl.ds(idx * d, d)]
    return jnp.where(is_ctx, ctx, lat)


def _rms(x, g):
    ms = jnp.mean(x * x, axis=-1, keepdims=True)
    return x * lax.rsqrt(ms + NORM_EPS) * g


def _ctx_rows(i, bm, n_lat):
    row = i * bm + lax.broadcasted_iota(jnp.int32, (bm, 1), 0)
    return row >= n_lat


def _even_pre_kernel(n_lat, x_ref, m_ref, g1_ref, w_in_ref, qg_ref, kvg_ref, w_uq_ref, w_uk_ref,
                     w_uv_ref, gq_ref, gk_ref, msq_ref, ct_ref, s1_ref, s2_ref,
                     hy_ref, q_ref, k_ref, v_ref):
    b = pl.program_id(0)
    i = pl.program_id(1)
    bm = x_ref.shape[0]
    is_ctx = _ctx_rows(i, bm, n_lat)
    x = x_ref[...]
    h = _rms(x, g1_ref[...])
    h = h * (1.0 + _mod_rows(m_ref, 1, b, is_ctx)) + _mod_rows(m_ref, 0, b, is_ctx)
    u = jnp.dot(h.astype(BF16), w_in_ref[...], preferred_element_type=F32)
    i_q = 3 * D_HY
    i_kv = i_q + MLA_Q_RANK
    i_kr = i_kv + MLA_KV_RANK
    hy_ref[...] = u[:, :i_q].astype(BF16)

    ct = ct_ref[...]
    s1 = s1_ref[...]
    s2 = s2_ref[...]
    msq = msq_ref[...]

    def head_norm(t, g):
        ms = jnp.dot((t * t).astype(BF16), msq, preferred_element_type=F32)
        return t * lax.rsqrt(ms + NORM_EPS) * g

    def rope(t):
        return (t * ct + pltpu.roll(t, HEAD_LANES - ROPE_AXIS // 2, 1) * s1
                + pltpu.roll(t, ROPE_AXIS // 2, 1) * s2)

    qn = _rms(u[:, i_q:i_kv], qg_ref[...]).astype(BF16)
    kvn = _rms(u[:, i_kv:i_kr], kvg_ref[...]).astype(BF16)
    gq = gq_ref[...]
    gk = gk_ref[...]
    k_rope = rope(head_norm(u[:, i_kr:], gk))
    for hd in range(MLA_HEADS):
        sl = slice(hd * HEAD_LANES, (hd + 1) * HEAD_LANES)
        qh = jnp.dot(qn, w_uq_ref[:, sl], preferred_element_type=F32)
        q_ref[hd] = rope(head_norm(qh, gq)).T.astype(BF16)
        kh = jnp.dot(kvn, w_uk_ref[:, sl], preferred_element_type=F32)
        k_ref[hd] = (head_norm(kh, gk) + k_rope).astype(BF16)
    v_t = jnp.dot(kvn, w_uv_ref[...], preferred_element_type=F32).T
    ones = jnp.ones((MLA_VA - MLA_V, bm), F32)
    for hd in range(MLA_HEADS):
        v_ref[hd] = jnp.concatenate([v_t[hd * MLA_V:(hd + 1) * MLA_V, :], ones], axis=0).astype(BF16)


def even_pre(x_all, m, g1, w, tabs, n_lat, bm):
    B, T, D = x_all.shape
    H = MLA_HEADS
    row = lambda n: pl.BlockSpec((None, bm, n), lambda b, i: (b, i, 0))
    tab = pl.BlockSpec((bm, HEAD_LANES), lambda b, i: (i, 0))
    headed = lambda nh: pl.BlockSpec((None, nh, bm, HEAD_LANES), lambda b, i: (b, 0, i, 0))
    consts = [m, g1, w["w_in"], w["q_norm_g"], w["kv_norm_g"], w["w_uq"], w["w_uk"], w["w_uv"],
              w["gq"], w["gk"], tabs["msq"]]
    return pl.pallas_call(
        functools.partial(_even_pre_kernel, n_lat),
        grid=(B, T // bm),
        in_specs=[row(D)] + [_const_spec(c.shape) for c in consts] + [tab, tab, tab],
        out_specs=[
            row(3 * D_HY),
            pl.BlockSpec((None, H, HEAD_LANES, bm), lambda b, i: (b, 0, 0, i)),
            headed(H),
            pl.BlockSpec((None, H, None, MLA_VA, bm), lambda b, i: (b, 0, i, 0, 0)),
        ],
        out_shape=[
            jax.ShapeDtypeStruct((B, T, 3 * D_HY), BF16),
            jax.ShapeDtypeStruct((B, H, HEAD_LANES, T), BF16),
            jax.ShapeDtypeStruct((B, H, T, HEAD_LANES), BF16),
            jax.ShapeDtypeStruct((B, H, T // bm, MLA_VA, bm), BF16),
        ],
        compiler_params=_cparams("parallel", "parallel"),
        name="even_pre",
    )(x_all, *consts, tabs["ct"], tabs["s1"], tabs["s2"])


def _attn_kernel(k_ref, qt_ref, vt_ref, o_ref, s_ref, mx_ref):
    n_chunks = k_ref.shape[1]
    tq = qt_ref.shape[2]
    assert n_chunks % 2 == 1

    def scores(c, slot):
        for hh in range(2):
            s = jnp.dot(k_ref[hh, c], qt_ref[hh], preferred_element_type=F32)
            s_ref[slot, hh] = s
            mx_ref[slot, hh] = jnp.max(s, axis=0, keepdims=True)

    def absorb(c, slot, stats):
        out = []
        for hh in range(2):
            m, acc = stats[hh]
            m_new = jnp.maximum(m, mx_ref[slot, hh])
            p = jnp.exp2(s_ref[slot, hh] - m_new)
            acc = jnp.exp2(m - m_new) * acc + jnp.dot(vt_ref[hh, c], p.astype(BF16), preferred_element_type=F32)
            out.append((m_new, acc))
        return tuple(out)

    def body(i, stats):
        c = 2 * i
        scores(c + 1, 1)
        stats = absorb(c, 0, stats)
        scores(c + 2, 0)
        return absorb(c + 1, 1, stats)

    init = tuple((jnp.full((1, tq), -jnp.inf, F32), jnp.zeros((MLA_VA, tq), F32)) for _ in range(2))
    scores(0, 0)
    stats = lax.fori_loop(0, n_chunks // 2, body, init)
    res = absorb(n_chunks - 1, 0, stats)
    out_t = jnp.concatenate([acc[:MLA_V] / acc[MLA_V:MLA_V + 1] for _, acc in res], axis=0)
    o_ref[...] = out_t.T.astype(o_ref.dtype)


def attention(k_chunks, q_t, v_t, q_blk0, n_q, tq):
    B, H, n_chunks, tk, _ = k_chunks.shape
    return pl.pallas_call(
        _attn_kernel,
        grid=(B, H // 2, n_q // tq),
        in_specs=[
            pl.BlockSpec((None, 2, n_chunks, tk, HEAD_LANES), lambda b, h, i: (b, h, 0, 0, 0)),
            pl.BlockSpec((None, 2, HEAD_LANES, tq), lambda b, h, i: (b, h, 0, q_blk0 + i)),
            pl.BlockSpec((None, 2, n_chunks, MLA_VA, tk), lambda b, h, i: (b, h, 0, 0, 0)),
        ],
        out_specs=pl.BlockSpec((None, tq, 2 * MLA_V), lambda b, h, i: (b, i, h)),
        out_shape=jax.ShapeDtypeStruct((B, n_q, H * MLA_V), BF16),
        scratch_shapes=[pltpu.VMEM((2, 2, tk, tq), F32), pltpu.VMEM((2, 2, 1, tq), F32)],
        compiler_params=_cparams("parallel", "parallel", "parallel"),
        name="mla_attention",
    )(k_chunks, q_t, v_t)


def mla_attend(q_t, k, v_t, n_lat, tq=1024):
    B, H, T, _ = k.shape
    n_chunks, tk = v_t.shape[2], v_t.shape[4]
    n_ctx = T - n_lat
    att_l = attention(k.reshape(B, H, n_chunks, tk, HEAD_LANES), q_t, v_t, 0, n_lat, tq)
    k_c = k[:, :, n_lat:].reshape(B, H, 1, n_ctx, HEAD_LANES)
    v_c = v_t[:, :, n_chunks - 1, :, tk - n_ctx:].reshape(B, H, 1, MLA_VA, n_ctx)
    att_c = attention(k_c, q_t, v_c, n_lat // n_ctx, n_ctx, n_ctx)
    return jnp.concatenate([att_l, att_c], axis=1)


def _post_kernel(n_mix, n_lat, tf, *refs):
    x_ref, m_ref, g2_ref = refs[:3]
    mix = refs[3:3 + 2 * n_mix]
    wg_ref, wu_ref, wd_ref, o_ref = refs[3 + 2 * n_mix:]
    b = pl.program_id(0)
    i = pl.program_id(1)
    bm = x_ref.shape[0]
    is_ctx = _ctx_rows(i, bm, n_lat)
    y = None
    for j in range(n_mix):
        t = jnp.dot(mix[2 * j][...], mix[2 * j + 1][...], preferred_element_type=F32)
        y = t if y is None else y + t
    x1 = x_ref[...] + _mod_rows(m_ref, 2, b, is_ctx) * y
    h = _rms(x1, g2_ref[...])
    h = (h * (1.0 + _mod_rows(m_ref, 4, b, is_ctx)) + _mod_rows(m_ref, 3, b, is_ctx)).astype(BF16)
    acc = None
    for f0 in range(0, D_FF, tf):
        g = jnp.dot(h, wg_ref[:, f0:f0 + tf], preferred_element_type=F32)
        u = jnp.dot(h, wu_ref[:, f0:f0 + tf], preferred_element_type=F32)
        a = (g * jax.nn.sigmoid(g) * u).astype(BF16)
        t = jnp.dot(a, wd_ref[f0:f0 + tf, :], preferred_element_type=F32)
        acc = t if acc is None else acc + t
    o_ref[...] = x1 + _mod_rows(m_ref, 5, b, is_ctx) * acc


def post_mixer(x_all, m, g2, mixes, wg, wu, wd, n_lat, n_rows, bm, tf=1408):
    B, T, D = x_all.shape
    row = lambda n: pl.BlockSpec((None, bm, n), lambda b, i: (b, i, 0))
    mix_args, mix_specs = [], []
    for a, wmat in mixes:
        mix_args += [a, wmat]
        mix_specs += [row(a.shape[-1]), _const_spec(wmat.shape, True)]
    return pl.pallas_call(
        functools.partial(_post_kernel, len(mixes), n_lat, tf),
        grid=(B, n_rows // bm),
        in_specs=[row(D), _const_spec(m.shape), _const_spec(g2.shape)] + mix_specs
        + [_const_spec(wg.shape, True), _const_spec(wu.shape, True), _const_spec(wd.shape, True)],
        out_specs=row(D),
        out_shape=jax.ShapeDtypeStruct((B, n_rows, D), F32),
        compiler_params=_cparams("parallel", "parallel"),
        name="post_mixer",
    )(x_all, m, g2, *mix_args, wg, wu, wd)


def _odd_pre_kernel(n_lat, x_ref, m_ref, g1_ref, w_ref, gate_ref, xr_ref):
    b = pl.program_id(0)
    i = pl.program_id(1)
    bm = x_ref.shape[0]
    is_ctx = _ctx_rows(i, bm, n_lat)
    h = _rms(x_ref[...], g1_ref[...])
    h = h * (1.0 + _mod_rows(m_ref, 1, b, is_ctx)) + _mod_rows(m_ref, 0, b, is_ctx)
    u = jnp.dot(h.astype(BF16), w_ref[...], preferred_element_type=F32)
    gate_ref[...] = jax.nn.gelu(u[:, :D_RNN]).astype(BF16)
    xr_ref[...] = u[:, D_RNN:]


def odd_pre(x_all, m, g1, w_in, n_lat, bm):
    B, T, D = x_all.shape
    row = lambda n: pl.BlockSpec((None, bm, n), lambda b, i: (b, i, 0))
    return pl.pallas_call(
        functools.partial(_odd_pre_kernel, n_lat),
        grid=(B, T // bm),
        in_specs=[row(D), _const_spec(m.shape), _const_spec(g1.shape), _const_spec(w_in.shape)],
        out_specs=[row(D_RNN), row(D_RNN)],
        out_shape=[jax.ShapeDtypeStruct((B, T, D_RNN), BF16), jax.ShapeDtypeStruct((B, T, D_RNN), F32)],
        compiler_params=_cparams("parallel", "parallel"),
        name="odd_pre",
    )(x_all, m, g1, w_in)


def _rglru_kernel(n_lat, chunk, gate_ref, xr_ref, cw_ref, cb_ref, wg_ref, bg_ref, lam_ref, o_ref, hf_ref, xc_ref):
    T = xr_ref.shape[0]
    n_ctx = T - n_lat
    R = chunk
    nv = R // 8
    cw = cw_ref[...]
    cb = cb_ref[...]
    sub = lax.broadcasted_iota(jnp.int32, (nv, 8, RG_BW), 1)

    def conv_chunk(c0, seg_lo, seg_hi):
        lo = jnp.maximum(c0 - 8, 0)
        hi = jnp.minimum(c0 + R, T - 8)
        win = jnp.concatenate([xr_ref[pl.ds(pl.multiple_of(lo, 8), 8), :],
                               xr_ref[pl.ds(pl.multiple_of(c0, 8), R), :],
                               xr_ref[pl.ds(pl.multiple_of(hi, 8), 8), :]], axis=0)
        g = c0 - 8 + lax.broadcasted_iota(jnp.int32, (R + 16, 1), 0)
        win = jnp.where((g >= seg_lo) & (g < seg_hi), win, 0.0)
        y = cb
        for kk in range(RG_CONV):
            off = 8 + kk - RG_CONV // 2
            y = y + win[off:off + R, :] * cw[kk:kk + 1, :]
        return y

    def gates(xc, d):
        t = jnp.tanh(jnp.dot(xc.astype(BF16), wg_ref[d], preferred_element_type=F32) + bg_ref[d])
        ig = 0.5 * t[:, RG_BW:] + 0.5
        c_half = (-0.5 * RG_C) * jax.nn.softplus(-lam_ref[d])
        log_a = t[:, :RG_BW] * c_half + c_half
        a = jnp.exp(log_a)
        bt = jnp.sqrt(-jnp.tanh(log_a) * (a * a + 1.0)) * (ig * xc)
        return a, bt

    def local_scan(a, bt, reverse):
        a = a.reshape(nv, 8, RG_BW)
        bt = bt.reshape(nv, 8, RG_BW)
        for d in (1, 2, 4):
            if reverse:
                keep = sub < 8 - d
                sh = 8 - d
            else:
                keep = sub >= d
                sh = d
            a_s = jnp.where(keep, pltpu.roll(a, sh, 1), 1.0)
            b_s = jnp.where(keep, pltpu.roll(bt, sh, 1), 0.0)
            bt = a * b_s + bt
            a = a * a_s
        return a, bt

    def chunk_scan(a, bt, h, reverse):
        a, bt = local_scan(a, bt, reverse)
        rows = [None] * nv
        order = range(nv - 1, -1, -1) if reverse else range(nv)
        edge = 0 if reverse else 7
        for v in order:
            hv = bt[v] + a[v] * h
            rows[v] = hv
            h = hv[edge:edge + 1, :]
        return jnp.concatenate(rows, axis=0), h

    n_lat_chunks = n_lat // R
    h0 = jnp.zeros((1, RG_BW), F32)

    def fwd_chunk(c0, seg_lo, seg_hi, h):
        xc = conv_chunk(c0, seg_lo, seg_hi)
        a, bt = gates(xc, 0)
        hs, h = chunk_scan(a, bt, h, False)
        rs = pl.ds(pl.multiple_of(c0, 8), R)
        hf_ref[rs, :] = hs
        xc_ref[rs, :] = xc
        return h

    h = h0
    for c in range(n_ctx // R):
        h = fwd_chunk(n_lat + c * R, n_lat, T, h)
    h = lax.fori_loop(0, n_lat_chunks, lambda c, hh: fwd_chunk(c * R, 0, n_lat, hh), h)

    def bwd_chunk(c0, h):
        rs = pl.ds(pl.multiple_of(c0, 8), R)
        a, bt = gates(xc_ref[rs, :], 1)
        hs, h = chunk_scan(a, bt, h, True)
        o_ref[rs, :] = (gate_ref[rs, :].astype(F32) * (hf_ref[rs, :] + hs)).astype(o_ref.dtype)
        return h

    h = h0
    for c in range(n_ctx // R - 1, -1, -1):
        h = bwd_chunk(n_lat + c * R, h)
    lax.fori_loop(0, n_lat_chunks, lambda c, hh: bwd_chunk((n_lat_chunks - 1 - c) * R, hh), h)


def rglru(gate, xr, cw, cb, wg, bg, lam, n_lat, chunk=256):
    B, T, _ = xr.shape
    col = pl.BlockSpec((None, T, RG_BW), lambda b, n: (b, 0, n))
    return pl.pallas_call(
        functools.partial(_rglru_kernel, n_lat, chunk),
        grid=(B, RG_BLOCKS),
        in_specs=[
            col, col,
            pl.BlockSpec((RG_CONV, RG_BW), lambda b, n: (0, n)),
            pl.BlockSpec((1, RG_BW), lambda b, n: (0, n)),
            pl.BlockSpec((None, 2, RG_BW, 2 * RG_BW), lambda b, n: (n, 0, 0, 0)),
            pl.BlockSpec((None, 2, 1, 2 * RG_BW), lambda b, n: (n, 0, 0, 0)),
            pl.BlockSpec((None, 2, 1, RG_BW), lambda b, n: (n, 0, 0, 0)),
        ],
        out_specs=col,
        out_shape=jax.ShapeDtypeStruct((B, T, D_RNN), BF16),
        scratch_shapes=[pltpu.VMEM((T, RG_BW), F32), pltpu.VMEM((T, RG_BW), F32)],
        compiler_params=_cparams("parallel", "parallel"),
        name="rglru",
    )(gate, xr, cw, cb, wg, bg, lam)


def _hyena_filters(L, w1, b1, w2, b2, w3, freq):
    k = jnp.arange(L, dtype=F32)
    t = (k / max(L - 1, 1))[:, None]
    ang = (2.0 * math.pi / L) * k[:, None] * jnp.linspace(1e-4, HY_BANDS - 1, HY_BANDS, dtype=F32)[None, :]
    z = jnp.concatenate([t, jnp.cos(ang), -jnp.sin(ang)], axis=-1)
    mm = functools.partial(jnp.dot, precision=lax.Precision.HIGHEST)
    h = jnp.sin(freq * (mm(z, w1) + b1))
    h = jnp.sin(freq * (mm(h, w2) + b2))
    h = mm(h, w3).reshape(L, 2, D_HY)
    half = L // 2
    dist = jnp.abs(k - half) / max(half, 1)
    deltas = jnp.abs(jnp.linspace(math.log(HY_TARGET) / HY_SLOW_DECAY, math.log(HY_TARGET) / HY_FAST_DECAY,
                                  D_HY, dtype=F32))
    window = jnp.exp(-dist[:, None] * deltas[None, :])
    return jnp.transpose(h * window[:, None, :], (1, 0, 2))


def _swap_neg(x):
    return jnp.concatenate([x[:, HY_CB:], -x[:, :HY_CB]], axis=1)


def _cstack(x):
    return jnp.concatenate([x, _swap_neg(x)], axis=0)


def _cmul(h, w):
    hre = jnp.concatenate([h[:, :HY_CB], h[:, :HY_CB]], axis=1)
    him = jnp.concatenate([-h[:, HY_CB:], h[:, HY_CB:]], axis=1)
    wsw = jnp.concatenate([w[:, HY_CB:], w[:, :HY_CB]], axis=1)
    return hre * w + him * wsw


def _hy_short_kernel(u_ref, w_ref, b_ref, o_ref):
    L = u_ref.shape[1]
    row = lax.broadcasted_iota(jnp.int32, (L, 1), 0)
    w = w_ref[...]
    halves = []
    for b in range(2):
        u = u_ref[b].astype(F32)
        prev = jnp.where(row >= 1, pltpu.roll(u, 1, 0), 0.0)
        nxt = jnp.where(row < L - 1, pltpu.roll(u, L - 1, 0), 0.0)
        halves.append(prev * w[0:1] + u * w[1:2] + nxt * w[2:3] + b_ref[...])
    o_ref[...] = jnp.concatenate(halves, axis=1).astype(o_ref.dtype)


def hy_short(u_hy, short_w, short_b, row_blk, L):
    return pl.pallas_call(
        _hy_short_kernel,
        grid=(3, HY_NCB),
        in_specs=[
            pl.BlockSpec((2, L, HY_CB), lambda g, cb: (0, row_blk, g * HY_NCB + cb)),
            pl.BlockSpec((3, HY_CB), lambda g, cb: (0, g * HY_NCB + cb)),
            pl.BlockSpec((1, HY_CB), lambda g, cb: (0, g * HY_NCB + cb)),
        ],
        out_specs=pl.BlockSpec((None, None, L, 2 * HY_CB), lambda g, cb: (g, cb, 0, 0)),
        out_shape=jax.ShapeDtypeStruct((3, HY_NCB, L, 2 * HY_CB), BF16),
        compiler_params=_cparams("parallel", "parallel"),
        name="hy_short",
    )(u_hy, short_w, short_b)


def _hy_conv_kernel(z_ref, h_ref, ma_ref, fb_ref, fc_ref, md_ref, o_ref, work, spec):
    n1 = z_ref.shape[0] // HY_N
    unroll = 8

    def put(rows, val):
        work[0, rows, :] = val[:, :HY_CB]
        work[1, rows, :] = val[:, HY_CB:]

    def get(rows):
        return jnp.concatenate([work[0, rows, :], work[1, rows, :]], axis=1).astype(BF16)

    def stage_a(src_ref):
        def step(i, carry):
            for g in range(unroll):
                j2 = i * unroll + g
                x = src_ref[pl.ds(pl.multiple_of(j2 * n1, 16), n1), :]
                a = jnp.dot(ma_ref[j2], _cstack(x), preferred_element_type=F32)
                put(pl.ds(j2, HY_N, stride=HY_PITCH), a)
            return carry

        lax.fori_loop(0, HY_N // unroll, step, 0)

    def mid(filtered):
        def step(i, carry):
            for g in range(unroll):
                k1 = i * unroll + g
                slab = pl.ds(pl.multiple_of(k1 * HY_PITCH, 8), HY_N)
                rows = pl.ds(pl.multiple_of(k1 * HY_N, 16), HY_N)
                w = jnp.dot(fb_ref[...], _cstack(get(slab)), preferred_element_type=F32)
                if filtered:
                    y = _cmul(spec[rows, :].astype(F32), w)
                    put(slab, jnp.dot(fc_ref[...], _cstack(y.astype(BF16)), preferred_element_type=F32))
                else:
                    spec[rows, :] = w.astype(spec.dtype)
            return carry

        lax.fori_loop(0, HY_N // unroll, step, 0)

    def stage_d(i, carry):
        for g in range(unroll):
            q2 = i * unroll + g
            y = jnp.dot(md_ref[q2], _cstack(get(pl.ds(q2, HY_N, stride=HY_PITCH))), preferred_element_type=F32)
            o_ref[pl.ds(pl.multiple_of(q2 * n1, 16), n1), :] = y.astype(o_ref.dtype)
        return carry

    stage_a(h_ref)
    mid(False)
    stage_a(z_ref)
    mid(True)
    lax.fori_loop(0, HY_N // unroll, stage_d, 0)


def hy_conv(z_t, h_t, tabs):
    ncb, L, lanes = z_t.shape
    blk = pl.BlockSpec((None, L, lanes), lambda cb: (cb, 0, 0), pipeline_mode=pl.Buffered(1))
    consts = [tabs["ma"], tabs["fb"], tabs["fc"], tabs["md"]]
    return pl.pallas_call(
        _hy_conv_kernel,
        grid=(ncb,),
        in_specs=[blk, blk] + [_const_spec(c.shape, True) for c in consts],
        out_specs=pl.BlockSpec((None, L, lanes), lambda cb: (cb, 0, 0)),
        out_shape=jax.ShapeDtypeStruct((ncb, L, lanes), BF16),
        scratch_shapes=[pltpu.VMEM((2, HY_N * HY_PITCH, HY_CB), F32), pltpu.VMEM((HY_N * HY_N, lanes), BF16)],
        compiler_params=_cparams("parallel"),
        name="hy_conv",
    )(z_t, h_t, *consts)


def _hy_gate_kernel(unpack, x_ref, y_ref, z_ref, b_ref, o_ref):
    z = x_ref[...].astype(F32) * (y_ref[...].astype(F32) + b_ref[...] * z_ref[...].astype(F32))
    if unpack:
        o_ref[0] = z[:, :HY_CB].astype(o_ref.dtype)
        o_ref[1] = z[:, HY_CB:].astype(o_ref.dtype)
    else:
        o_ref[...] = z.astype(o_ref.dtype)


def hy_gate(x, gx, y, z, gz, bias_p, unpack, rows=2048):
    ncb, L, lanes = y.shape
    blk = pl.BlockSpec((None, rows, lanes), lambda cb, r: (cb, r, 0))
    pick = lambda g: pl.BlockSpec((None, None, rows, lanes), lambda cb, r: (g, cb, r, 0))
    if unpack:
        out_spec = pl.BlockSpec((2, rows, HY_CB), lambda cb, r: (0, r, cb))
        out_shape = jax.ShapeDtypeStruct((2, L, ncb * HY_CB), BF16)
    else:
        out_spec, out_shape = blk, jax.ShapeDtypeStruct(y.shape, BF16)
    return pl.pallas_call(
        functools.partial(_hy_gate_kernel, unpack),
        grid=(ncb, L // rows),
        in_specs=[pick(gx), blk, pick(gz), pl.BlockSpec((None, 1, lanes), lambda cb, r: (cb, 0, 0))],
        out_specs=out_spec,
        out_shape=out_shape,
        compiler_params=_cparams("parallel", "parallel"),
        name="hy_gate",
    )(x, y, z, bias_p)


def _hy_ctx_kernel(s_ref, h_ref, b_ref, ff_ref, fi_ref, o_ref):
    z = s_ref[0].astype(F32)
    L = z.shape[0]
    ff = ff_ref[...]
    for o in range(2):
        hp = jnp.concatenate([h_ref[o], jnp.zeros((L, HY_CB), F32)], axis=1)
        spec = jnp.dot(ff, _cstack(hp.astype(BF16)), preferred_element_type=F32)
        w = jnp.dot(ff, _cstack(z.astype(BF16)), preferred_element_type=F32)
        y = jnp.dot(fi_ref[...], _cstack(_cmul(spec, w).astype(BF16)), preferred_element_type=F32)
        z = s_ref[o + 1].astype(F32) * (y + b_ref[o] * z)
    o_ref[0] = z[:, :HY_CB].astype(o_ref.dtype)
    o_ref[1] = z[:, HY_CB:].astype(o_ref.dtype)


def hy_ctx(s, h, bias_p, ff, fi):
    _, ncb, L, _ = s.shape
    return pl.pallas_call(
        _hy_ctx_kernel,
        grid=(ncb,),
        in_specs=[
            pl.BlockSpec((3, None, L, 2 * HY_CB), lambda cb: (0, cb, 0, 0)),
            pl.BlockSpec((2, L, HY_CB), lambda cb: (0, 0, cb)),
            pl.BlockSpec((2, None, 1, 2 * HY_CB), lambda cb: (0, cb, 0, 0)),
            _const_spec(ff.shape), _const_spec(fi.shape),
        ],
        out_specs=pl.BlockSpec((2, L, HY_CB), lambda cb: (0, 0, cb)),
        out_shape=jax.ShapeDtypeStruct((2, L, ncb * HY_CB), BF16),
        compiler_params=_cparams("parallel"),
        name="hy_ctx",
    )(s, h, bias_p, ff, fi)


def _hy_tables(L, L_ctx):
    n = 2 * L
    n1 = L // HY_N
    two_pi = 2.0 * math.pi
    ar = lambda m: jnp.arange(m, dtype=jnp.int32)
    cs = lambda m, period: (jnp.cos(m.astype(F32) * (two_pi / period)), jnp.sin(m.astype(F32) * (two_pi / period)))
    tc, ts = cs((ar(HY_N)[:, None] * ar(HY_N)[None, :]) % n, n)
    c1, s1 = cs((ar(HY_N)[:, None] * ar(n1)[None, :]) % HY_N, HY_N)
    c = c1[None] * tc[:, :, None] - s1[None] * ts[:, :, None]
    s = s1[None] * tc[:, :, None] + c1[None] * ts[:, :, None]
    ma = jnp.concatenate([c, s], axis=-1).astype(BF16)
    q1 = ar(n1) + n1 // 2
    c1, s1 = cs((q1[:, None] * ar(HY_N)[None, :]) % HY_N, HY_N)
    c = c1[None] * tc[:, None, :] - s1[None] * ts[:, None, :]
    s = s1[None] * tc[:, None, :] + c1[None] * ts[:, None, :]
    md = (jnp.concatenate([c, -s], axis=-1) / n).astype(BF16)
    m = (ar(HY_N)[:, None] * ar(HY_N)[None, :]) % HY_N
    c, s = cs(m, HY_N)
    fb = jnp.concatenate([c, s], axis=-1).astype(BF16)
    fc = jnp.concatenate([c, -s], axis=-1).astype(BF16)
    nc = 2 * L_ctx
    m = (ar(nc)[:, None] * ar(L_ctx)[None, :]) % nc
    c, s = cs(m, nc)
    ff = jnp.concatenate([c, s], axis=-1).astype(BF16)
    m = ((ar(L_ctx) + L_ctx // 2)[:, None] * ar(nc)[None, :]) % nc
    c, s = cs(m, nc)
    fi = (jnp.concatenate([c, -s], axis=-1) / nc).astype(BF16)
    return {"ma": ma, "md": md, "fb": fb, "fc": fc, "ff": ff, "fi": fi}


def hyena(u_hy, short_w, short_b, filt, bias, tabs, n_lat, n_ctx):
    bias_p = jnp.concatenate([bias.reshape(2, HY_NCB, 1, HY_CB)] * 2, axis=-1)
    n1 = n_lat // HY_N
    to_slabs = lambda t: jnp.swapaxes(t.reshape(HY_NCB, n1, HY_N, -1), 1, 2).reshape(HY_NCB, n_lat, -1)
    to_time = lambda t: jnp.swapaxes(t.reshape(HY_NCB, HY_N, n1, -1), 1, 2).reshape(HY_NCB, n_lat, -1)
    h_lat = _hyena_filters(n_lat, *filt)
    h_lat = jnp.transpose(h_lat.reshape(2, n1, HY_N, HY_NCB, HY_CB), (0, 3, 2, 1, 4)).reshape(2, HY_NCB, n_lat, HY_CB)
    h_lat = jnp.concatenate([h_lat, jnp.zeros_like(h_lat)], axis=-1).astype(BF16)
    s = hy_short(u_hy, short_w, short_b, 0, n_lat)
    z, gz = s, 0
    for o in range(2):
        y = hy_conv(to_slabs(z[gz]), h_lat[o], tabs)
        z, gz = hy_gate(s, o + 1, to_time(y), z, gz, bias_p[o], o == 1)[None], 0
    s_c = hy_short(u_hy, short_w, short_b, n_lat // n_ctx, n_ctx)
    z_c = hy_ctx(s_c, _hyena_filters(n_ctx, *filt), bias_p, tabs["ff"], tabs["fi"])
    return jnp.concatenate([z[0], z_c], axis=1)


def _rope_tables(n_lat, n_ctx):
    rows = n_lat // GRID_W
    inv_freq = ROPE_BASE ** (-jnp.arange(0, ROPE_AXIS, 2, dtype=F32) / ROPE_AXIS)
    ang_r = jnp.arange(rows, dtype=F32)[:, None] * inv_freq[None, :]
    ang_c = jnp.arange(GRID_W, dtype=F32)[:, None] * inv_freq[None, :]
    nf = ROPE_AXIS // 2
    per_row = lambda t: jnp.broadcast_to(t[:, None, :], (rows, GRID_W, nf)).reshape(n_lat, nf)
    per_col = lambda t: jnp.broadcast_to(t[None, :, :], (rows, GRID_W, nf)).reshape(n_lat, nf)
    cos = jnp.concatenate([per_row(jnp.cos(ang_r)), per_col(jnp.cos(ang_c))], axis=-1)
    sin = jnp.concatenate([per_row(jnp.sin(ang_r)), per_col(jnp.sin(ang_c))], axis=-1)
    cos = jnp.concatenate([cos, jnp.ones((n_ctx, ROPE_AXIS), F32)], axis=0)
    sin = jnp.concatenate([sin, jnp.zeros((n_ctx, ROPE_AXIS), F32)], axis=0)
    T = n_lat + n_ctx
    z8 = jnp.zeros((T, ROPE_AXIS // 2), F32)
    cos32 = jnp.concatenate([cos[:, :8], cos[:, :8], cos[:, 8:], cos[:, 8:]], axis=-1)
    first = jnp.concatenate([-sin[:, :8], z8, -sin[:, 8:], z8], axis=-1)
    second = jnp.concatenate([z8, sin[:, :8], z8, sin[:, 8:]], axis=-1)
    pad = lambda t, fill: jnp.concatenate(
        [jnp.full((T, MLA_NOPE), fill, F32), t, jnp.zeros((T, HEAD_LANES - MLA_NOPE - MLA_ROPE), F32)], axis=-1)
    lane = jnp.arange(HEAD_LANES)
    seg = jnp.where(lane < MLA_NOPE, 0, jnp.where(lane < MLA_NOPE + MLA_ROPE, 1, 2))
    msq = (seg[:, None] == seg[None, :]) & (seg[:, None] < 2)
    msq = jnp.where(msq, jnp.where(seg[:, None] == 0, 1.0 / MLA_NOPE, 1.0 / MLA_ROPE), 0.0).astype(BF16)
    return {"ct": pad(cos32, 1.0), "s1": pad(first, 0.0), "s2": pad(second, 0.0), "msq": msq}


def _head_gain(g, scale):
    return jnp.concatenate([g * scale, jnp.zeros((HEAD_LANES - g.shape[0],), F32)])[None, :]


def _even_weights(j, ev_w_in, mla_q_norm_g, mla_w_uq, mla_kv_norm_g, mla_w_ukv, mla_qk_g_q, mla_qk_g_k):
    i_q = 3 * D_HY
    i_kv = i_q + MLA_Q_RANK
    i_kr = i_kv + MLA_KV_RANK
    w_in = ev_w_in[j]
    d = w_in.shape[0]
    kr_group = jnp.concatenate([jnp.zeros((d, MLA_NOPE), F32), w_in[:, i_kr:],
                                jnp.zeros((d, HEAD_LANES - MLA_NOPE - MLA_ROPE), F32)], axis=-1)
    w_in_p = jnp.concatenate([w_in[:, :i_kr], kr_group], axis=-1).astype(BF16)
    hq = MLA_NOPE + MLA_ROPE
    w_uq = mla_w_uq[j].reshape(MLA_Q_RANK, MLA_HEADS, hq)
    w_uq = jnp.pad(w_uq, ((0, 0), (0, 0), (0, HEAD_LANES - hq))).reshape(MLA_Q_RANK, -1).astype(BF16)
    w_ukv = mla_w_ukv[j].reshape(MLA_KV_RANK, MLA_HEADS, MLA_NOPE + MLA_V)
    w_uk = jnp.pad(w_ukv[:, :, :MLA_NOPE], ((0, 0), (0, 0), (0, HEAD_LANES - MLA_NOPE)))
    w_uk = w_uk.reshape(MLA_KV_RANK, -1).astype(BF16)
    w_uv = w_ukv[:, :, MLA_NOPE:].reshape(MLA_KV_RANK, -1).astype(BF16)
    return {
        "w_in": w_in_p, "q_norm_g": mla_q_norm_g[j][None, :], "kv_norm_g": mla_kv_norm_g[j][None, :],
        "w_uq": w_uq, "w_uk": w_uk, "w_uv": w_uv,
        "gq": _head_gain(mla_qk_g_q[j], MLA_SCALE * math.log2(math.e)), "gk": _head_gain(mla_qk_g_k[j], 1.0),
    }


def _rg_gate_weights(j, rg_w_a, rg_b_a, rg_w_x, rg_b_x, rg_lambda):
    wg = 0.5 * jnp.concatenate([rg_w_a[j], rg_w_x[j]], axis=-1)
    wg = jnp.transpose(wg, (1, 0, 2, 3)).astype(BF16)
    blk = lambda t: jnp.transpose(t.reshape(2, RG_BLOCKS, RG_BW), (1, 0, 2))
    bg = 0.5 * jnp.concatenate([blk(rg_b_a[j]), blk(rg_b_x[j])], axis=-1)[:, :, None, :]
    lam = blk(rg_lambda[j])[:, :, None, :]
    return wg, bg, lam


def kernel(x, c, ctx, c_ctx, ada_w, ada_b, norm1_g, norm2_g, ffn_w_in, ffn_w_out, ev_w_in, hy_short_w, hy_short_b, hy_w1, hy_b1, hy_w2, hy_b2, hy_w3, hy_freq, hy_bias, mla_q_norm_g, mla_w_uq, mla_kv_norm_g, mla_w_ukv, mla_qk_g_q, mla_qk_g_k, ev_w_out, rg_w_in, rg_conv_w, rg_conv_b, rg_w_a, rg_b_a, rg_w_x, rg_b_x, rg_lambda, rg_w_out):
    B, n_lat, D = x.shape
    n_ctx = ctx.shape[1]
    T = n_lat + n_ctx
    bm = 768
    assert T % bm == 0 and n_lat % 512 == 0 and n_ctx == 256

    cond8 = jnp.concatenate([c, c_ctx[None, :], jnp.zeros((8 - B - 1, D), F32)], axis=0)
    mods = ada_modulation(cond8, ada_w, ada_b)
    tabs = _rope_tables(n_lat, n_ctx)
    hy_tabs = _hy_tables(n_lat, n_ctx)
    x_all = jnp.concatenate([x, ctx], axis=1)

    for i in range(DEPTH):
        j = i // 2
        last = i == DEPTH - 1
        m = mods[i]
        g1 = norm1_g[i][None, :]
        g2 = norm2_g[i][None, :]
        if i % 2 == 0:
            w = _even_weights(j, ev_w_in, mla_q_norm_g, mla_w_uq, mla_kv_norm_g, mla_w_ukv, mla_qk_g_q, mla_qk_g_k)
            u_hy, q, k, v = even_pre(x_all, m, g1, w, tabs, n_lat, bm)
            att = mla_attend(q, k, v, n_lat)
            filt = (hy_w1[j], hy_b1[j], hy_w2[j], hy_b2[j], hy_w3[j], hy_freq[j])
            hy = hyena(u_hy, hy_short_w[j], hy_short_b[j][None, :], filt, hy_bias[j], hy_tabs, n_lat, n_ctx)
            w_out = ev_w_out[j].astype(BF16)
            mixes = [(hy, w_out[:D_HY]), (att, w_out[D_HY:])]
        else:
            gate, xr = odd_pre(x_all, m, g1, rg_w_in[j].astype(BF16), n_lat, bm)
            wg, bg, lam = _rg_gate_weights(j, rg_w_a, rg_b_a, rg_w_x, rg_b_x, rg_lambda)
            y = rglru(gate, xr, rg_conv_w[j], rg_conv_b[j][None, :], wg, bg, lam, n_lat)
            mixes = [(y, rg_w_out[j].astype(BF16))]
        wg_f = ffn_w_in[i][:, :D_FF].astype(BF16)
        wu_f = ffn_w_in[i][:, D_FF:].astype(BF16)
        wd_f = ffn_w_out[i].astype(BF16)
        if last:
            x_all = post_mixer(x_all, m, g2, mixes, wg_f, wu_f, wd_f, n_lat, n_lat, 512)
        else:
            x_all = post_mixer(x_all, m, g2, mixes, wg_f, wu_f, wd_f, n_lat, T, bm)
    return x_all
```

```python
import functools
import math

import jax
import jax.numpy as jnp
from jax import lax
from jax.experimental import pallas as pl
from jax.experimental.pallas import tpu as pltpu

F32 = jnp.float32
BF16 = jnp.bfloat16

D_MODEL = 1024
DEPTH = 4
GRID_W = 64
D_FF = 2816
NORM_EPS = 1e-6

D_HY = 512
HY_CB = 128
HY_NCB = D_HY // HY_CB
HY_N = 128
HY_PITCH = HY_N + 8
HY_BANDS = 16
HY_FAST_DECAY = 0.3
HY_SLOW_DECAY = 1.5
HY_TARGET = 1e-2

MLA_HEADS = 8
MLA_NOPE = 64
MLA_ROPE = 32
MLA_V = 64
MLA_VA = MLA_V + 16
MLA_Q_RANK = 384
MLA_KV_RANK = 256
MLA_SCALE = (MLA_NOPE + MLA_ROPE) ** -0.5
ROPE_AXIS = 16
ROPE_BASE = 10000.0
HEAD_LANES = 128

D_RNN = 1280
RG_BLOCKS = 10
RG_BW = 128
RG_CONV = 4
RG_C = 8.0

EVEN_IN_PAD = 3 * D_HY + MLA_Q_RANK + MLA_KV_RANK + HEAD_LANES

VMEM_LIMIT = 56 * 1024 * 1024


def _cparams(*sem):
    return pltpu.CompilerParams(dimension_semantics=sem, vmem_limit_bytes=VMEM_LIMIT)


def _const_spec(shape, single_buffer=False):
    nd = len(shape)
    kw = {"pipeline_mode": pl.Buffered(1)} if single_buffer else {}
    return pl.BlockSpec(shape, lambda *_: (0,) * nd, **kw)


def _ada_kernel(s_ref, w_ref, b_ref, o_ref):
    s = s_ref[...]
    s = s * jax.nn.sigmoid(s)
    o_ref[...] = jnp.dot(s, w_ref[...], preferred_element_type=F32) + b_ref[...]


def ada_modulation(cond8, ada_w, ada_b):
    depth, d, n = ada_w.shape
    tn = 1536
    return pl.pallas_call(
        _ada_kernel,
        grid=(depth, n // tn),
        in_specs=[
            pl.BlockSpec((8, d), lambda i, j: (0, 0)),
            pl.BlockSpec((None, d, tn), lambda i, j: (i, 0, j)),
            pl.BlockSpec((None, 1, tn), lambda i, j: (i, 0, j)),
        ],
        out_specs=pl.BlockSpec((None, 8, tn), lambda i, j: (i, 0, j)),
        out_shape=jax.ShapeDtypeStruct((depth, 8, n), F32),
        compiler_params=_cparams("parallel", "parallel"),
        name="ada_modulation",
    )(cond8, ada_w, ada_b.reshape(depth, 1, n))


def _mod_rows(m_ref, idx, b, is_ctx):
    d = D_MODEL
    lat = m_ref[pl.ds(b, 1), pl.ds(idx * d, d)]
    ctx = m_ref[pl.ds(2, 1), pl.ds(idx * d, d)]
    return jnp.where(is_ctx, ctx, lat)


def _rms(x, g):
    ms = jnp.mean(x * x, axis=-1, keepdims=True)
    return x * lax.rsqrt(ms + NORM_EPS) * g


def _ctx_rows(i, bm, n_lat):
    row = i * bm + lax.broadcasted_iota(jnp.int32, (bm, 1), 0)
    return row >= n_lat


def _even_pre_kernel(n_lat, x_ref, m_ref, g1_ref, w_in_ref, qg_ref, kvg_ref, w_uq_ref, w_uk_ref,
                     w_uv_ref, gq_ref, gk_ref, msq_ref, ct_ref, s1_ref, s2_ref,
                     hy_ref, q_ref, k_ref, v_ref):
    b = pl.program_id(0)
    i = pl.program_id(1)
    bm = x_ref.shape[0]
    is_ctx = _ctx_rows(i, bm, n_lat)
    x = x_ref[...]
    h = _rms(x, g1_ref[...])
    h = h * (1.0 + _mod_rows(m_ref, 1, b, is_ctx)) + _mod_rows(m_ref, 0, b, is_ctx)
    u = jnp.dot(h.astype(BF16), w_in_ref[...], preferred_element_type=F32)
    i_q = 3 * D_HY
    i_kv = i_q + MLA_Q_RANK
    i_kr = i_kv + MLA_KV_RANK
    hy_ref[...] = u[:, :i_q].astype(BF16)

    ct = ct_ref[...]
    s1 = s1_ref[...]
    s2 = s2_ref[...]
    msq = msq_ref[...]

    def head_norm(t, g):
        ms = jnp.dot((t * t).astype(BF16), msq, preferred_element_type=F32)
        return t * lax.rsqrt(ms + NORM_EPS) * g

    def rope(t):
        return (t * ct + pltpu.roll(t, HEAD_LANES - ROPE_AXIS // 2, 1) * s1
                + pltpu.roll(t, ROPE_AXIS // 2, 1) * s2)

    qn = _rms(u[:, i_q:i_kv], qg_ref[...]).astype(BF16)
    kvn = _rms(u[:, i_kv:i_kr], kvg_ref[...]).astype(BF16)
    gq = gq_ref[...]
    gk = gk_ref[...]
    k_rope = rope(head_norm(u[:, i_kr:], gk))
    for hd in range(MLA_HEADS):
        sl = slice(hd * HEAD_LANES, (hd + 1) * HEAD_LANES)
        qh = jnp.dot(qn, w_uq_ref[:, sl], preferred_element_type=F32)
        q_ref[hd] = rope(head_norm(qh, gq)).T.astype(BF16)
        kh = jnp.dot(kvn, w_uk_ref[:, sl], preferred_element_type=F32)
        k_ref[hd] = (head_norm(kh, gk) + k_rope).astype(BF16)
    v_t = jnp.dot(kvn, w_uv_ref[...], preferred_element_type=F32).T
    ones = jnp.ones((MLA_VA - MLA_V, bm), F32)
    for hd in range(MLA_HEADS):
        v_ref[hd] = jnp.concatenate([v_t[hd * MLA_V:(hd + 1) * MLA_V, :], ones], axis=0).astype(BF16)


def even_pre(x_all, m, g1, w, tabs, n_lat, bm):
    B, T, D = x_all.shape
    H = MLA_HEADS
    row = lambda n: pl.BlockSpec((None, bm, n), lambda b, i: (b, i, 0))
    tab = pl.BlockSpec((bm, HEAD_LANES), lambda b, i: (i, 0))
    headed = lambda nh: pl.BlockSpec((None, nh, bm, HEAD_LANES), lambda b, i: (b, 0, i, 0))
    consts = [m, g1, w["w_in"], w["q_norm_g"], w["kv_norm_g"], w["w_uq"], w["w_uk"], w["w_uv"],
              w["gq"], w["gk"], tabs["msq"]]
    return pl.pallas_call(
        functools.partial(_even_pre_kernel, n_lat),
        grid=(B, T // bm),
        in_specs=[row(D)] + [_const_spec(c.shape) for c in consts] + [tab, tab, tab],
        out_specs=[
            row(3 * D_HY),
            pl.BlockSpec((None, H, HEAD_LANES, bm), lambda b, i: (b, 0, 0, i)),
            headed(H),
            pl.BlockSpec((None, H, None, MLA_VA, bm), lambda b, i: (b, 0, i, 0, 0)),
        ],
        out_shape=[
            jax.ShapeDtypeStruct((B, T, 3 * D_HY), BF16),
            jax.ShapeDtypeStruct((B, H, HEAD_LANES, T), BF16),
            jax.ShapeDtypeStruct((B, H, T, HEAD_LANES), BF16),
            jax.ShapeDtypeStruct((B, H, T // bm, MLA_VA, bm), BF16),
        ],
        compiler_params=_cparams("parallel", "parallel"),
        name="even_pre",
    )(x_all, *consts, tabs["ct"], tabs["s1"], tabs["s2"])


def _attn_kernel(k_ref, qt_ref, vt_ref, o_ref, s_ref, mx_ref):
    n_chunks = k_ref.shape[1]
    tq = qt_ref.shape[2]
    assert n_chunks % 2 == 1

    def scores(c, slot):
        for hh in range(2):
            s = jnp.dot(k_ref[hh, c], qt_ref[hh], preferred_element_type=F32)
            s_ref[slot, hh] = s
            mx_ref[slot, hh] = jnp.max(s, axis=0, keepdims=True)

    def absorb(c, slot, stats):
        out = []
        for hh in range(2):
            m, acc = stats[hh]
            m_new = jnp.maximum(m, mx_ref[slot, hh])
            p = jnp.exp2(s_ref[slot, hh] - m_new)
            acc = jnp.exp2(m - m_new) * acc + jnp.dot(vt_ref[hh, c], p.astype(BF16), preferred_element_type=F32)
            out.append((m_new, acc))
        return tuple(out)

    def body(i, stats):
        c = 2 * i
        scores(c + 1, 1)
        stats = absorb(c, 0, stats)
        scores(c + 2, 0)
        return absorb(c + 1, 1, stats)

    init = tuple((jnp.full((1, tq), -jnp.inf, F32), jnp.zeros((MLA_VA, tq), F32)) for _ in range(2))
    scores(0, 0)
    stats = lax.fori_loop(0, n_chunks // 2, body, init)
    res = absorb(n_chunks - 1, 0, stats)
    out_t = jnp.concatenate([acc[:MLA_V] / acc[MLA_V:MLA_V + 1] for _, acc in res], axis=0)
    o_ref[...] = out_t.T.astype(o_ref.dtype)


def attention(k_chunks, q_t, v_t, q_blk0, n_q, tq):
    B, H, n_chunks, tk, _ = k_chunks.shape
    return pl.pallas_call(
        _attn_kernel,
        grid=(B, H // 2, n_q // tq),
        in_specs=[
            pl.BlockSpec((None, 2, n_chunks, tk, HEAD_LANES), lambda b, h, i: (b, h, 0, 0, 0)),
            pl.BlockSpec((None, 2, HEAD_LANES, tq), lambda b, h, i: (b, h, 0, q_blk0 + i)),
            pl.BlockSpec((None, 2, n_chunks, MLA_VA, tk), lambda b, h, i: (b, h, 0, 0, 0)),
        ],
        out_specs=pl.BlockSpec((None, tq, 2 * MLA_V), lambda b, h, i: (b, i, h)),
        out_shape=jax.ShapeDtypeStruct((B, n_q, H * MLA_V), BF16),
        scratch_shapes=[pltpu.VMEM((2, 2, tk, tq), F32), pltpu.VMEM((2, 2, 1, tq), F32)],
        compiler_params=_cparams("parallel", "parallel", "parallel"),
        name="mla_attention",
    )(k_chunks, q_t, v_t)


def mla_attend(q_t, k, v_t, n_lat, tq=1024):
    B, H, T, _ = k.shape
    n_chunks, tk = v_t.shape[2], v_t.shape[4]
    n_ctx = T - n_lat
    att_l = attention(k.reshape(B, H, n_chunks, tk, HEAD_LANES), q_t, v_t, 0, n_lat, tq)
    k_c = k[:, :, n_lat:].reshape(B, H, 1, n_ctx, HEAD_LANES)
    v_c = v_t[:, :, n_chunks - 1, :, tk - n_ctx:].reshape(B, H, 1, MLA_VA, n_ctx)
    att_c = attention(k_c, q_t, v_c, n_lat // n_ctx, n_ctx, n_ctx)
    return jnp.concatenate([att_l, att_c], axis=1)


def _post_kernel(n_mix, n_lat, tf, *refs):
    x_ref, m_ref, g2_ref = refs[:3]
    mix = refs[3:3 + 2 * n_mix]
    wg_ref, wu_ref, wd_ref, o_ref = refs[3 + 2 * n_mix:]
    b = pl.program_id(0)
    i = pl.program_id(1)
    bm = x_ref.shape[0]
    is_ctx = _ctx_rows(i, bm, n_lat)
    y = None
    for j in range(n_mix):
        t = jnp.dot(mix[2 * j][...], mix[2 * j + 1][...], preferred_element_type=F32)
        y = t if y is None else y + t
    x1 = x_ref[...] + _mod_rows(m_ref, 2, b, is_ctx) * y
    h = _rms(x1, g2_ref[...])
    h = (h * (1.0 + _mod_rows(m_ref, 4, b, is_ctx)) + _mod_rows(m_ref, 3, b, is_ctx)).astype(BF16)
    acc = None
    for f0 in range(0, D_FF, tf):
        g = jnp.dot(h, wg_ref[:, f0:f0 + tf], preferred_element_type=F32)
        u = jnp.dot(h, wu_ref[:, f0:f0 + tf], preferred_element_type=F32)
        a = (g * jax.nn.sigmoid(g) * u).astype(BF16)
        t = jnp.dot(a, wd_ref[f0:f0 + tf, :], preferred_element_type=F32)
        acc = t if acc is None else acc + t
    o_ref[...] = x1 + _mod_rows(m_ref, 5, b, is_ctx) * acc


def post_mixer(x_all, m, g2, mixes, wg, wu, wd, n_lat, n_rows, bm, tf=1408):
    B, T, D = x_all.shape
    row = lambda n: pl.BlockSpec((None, bm, n), lambda b, i: (b, i, 0))
    mix_args, mix_specs = [], []
    for a, wmat in mixes:
        mix_args += [a, wmat]
        mix_specs += [row(a.shape[-1]), _const_spec(wmat.shape, True)]
    return pl.pallas_call(
        functools.partial(_post_kernel, len(mixes), n_lat, tf),
        grid=(B, n_rows // bm),
        in_specs=[row(D), _const_spec(m.shape), _const_spec(g2.shape)] + mix_specs
        + [_const_spec(wg.shape, True), _const_spec(wu.shape, True), _const_spec(wd.shape, True)],
        out_specs=row(D),
        out_shape=jax.ShapeDtypeStruct((B, n_rows, D), F32),
        compiler_params=_cparams("parallel", "parallel"),
        name="post_mixer",
    )(x_all, m, g2, *mix_args, wg, wu, wd)


def _odd_pre_kernel(n_lat, x_ref, m_ref, g1_ref, w_ref, gate_ref, xr_ref):
    b = pl.program_id(0)
    i = pl.program_id(1)
    bm = x_ref.shape[0]
    is_ctx = _ctx_rows(i, bm, n_lat)
    h = _rms(x_ref[...], g1_ref[...])
    h = h * (1.0 + _mod_rows(m_ref, 1, b, is_ctx)) + _mod_rows(m_ref, 0, b, is_ctx)
    u = jnp.dot(h.astype(BF16), w_ref[...], preferred_element_type=F32)
    gate_ref[...] = jax.nn.gelu(u[:, :D_RNN]).astype(BF16)
    xr_ref[...] = u[:, D_RNN:]


def odd_pre(x_all, m, g1, w_in, n_lat, bm):
    B, T, D = x_all.shape
    row = lambda n: pl.BlockSpec((None, bm, n), lambda b, i: (b, i, 0))
    return pl.pallas_call(
        functools.partial(_odd_pre_kernel, n_lat),
        grid=(B, T // bm),
        in_specs=[row(D), _const_spec(m.shape), _const_spec(g1.shape), _const_spec(w_in.shape)],
        out_specs=[row(D_RNN), row(D_RNN)],
        out_shape=[jax.ShapeDtypeStruct((B, T, D_RNN), BF16), jax.ShapeDtypeStruct((B, T, D_RNN), F32)],
        compiler_params=_cparams("parallel", "parallel"),
        name="odd_pre",
    )(x_all, m, g1, w_in)


def _rglru_kernel(n_lat, chunk, gate_ref, xr_ref, cw_ref, cb_ref, wg_ref, bg_ref, lam_ref, o_ref, hf_ref, xc_ref):
    T = xr_ref.shape[0]
    n_ctx = T - n_lat
    R = chunk
    nv = R // 8
    cw = cw_ref[...]
    cb = cb_ref[...]
    sub = lax.broadcasted_iota(jnp.int32, (nv, 8, RG_BW), 1)

    def conv_chunk(c0, seg_lo, seg_hi):
        lo = jnp.maximum(c0 - 8, 0)
        hi = jnp.minimum(c0 + R, T - 8)
        win = jnp.concatenate([xr_ref[pl.ds(pl.multiple_of(lo, 8), 8), :],
                               xr_ref[pl.ds(pl.multiple_of(c0, 8), R), :],
                               xr_ref[pl.ds(pl.multiple_of(hi, 8), 8), :]], axis=0)
        g = c0 - 8 + lax.broadcasted_iota(jnp.int32, (R + 16, 1), 0)
        win = jnp.where((g >= seg_lo) & (g < seg_hi), win, 0.0)
        y = cb
        for kk in range(RG_CONV):
            off = 8 + kk - RG_CONV // 2
            y = y + win[off:off + R, :] * cw[kk:kk + 1, :]
        return y

    def gates(xc, d):
        t = jnp.tanh(jnp.dot(xc.astype(BF16), wg_ref[d], preferred_element_type=F32) + bg_ref[d])
        ig = 0.5 * t[:, RG_BW:] + 0.5
        c_half = (-0.5 * RG_C) * jax.nn.softplus(-lam_ref[d])
        log_a = t[:, :RG_BW] * c_half + c_half
        a = jnp.exp(log_a)
        bt = jnp.sqrt(-jnp.tanh(log_a) * (a * a + 1.0)) * (ig * xc)
        return a, bt

    def local_scan(a, bt, reverse):
        a = a.reshape(nv, 8, RG_BW)
        bt = bt.reshape(nv, 8, RG_BW)
        for d in (1, 2, 4):
            if reverse:
                keep = sub < 8 - d
                sh = 8 - d
            else:
                keep = sub >= d
                sh = d
            a_s = jnp.where(keep, pltpu.roll(a, sh, 1), 1.0)
            b_s = jnp.where(keep, pltpu.roll(bt, sh, 1), 0.0)
            bt = a * b_s + bt
            a = a * a_s
        return a, bt

    def chunk_scan(a, bt, h, reverse):
        a, bt = local_scan(a, bt, reverse)
        rows = [None] * nv
        order = range(nv - 1, -1, -1) if reverse else range(nv)
        edge = 0 if reverse else 7
        for v in order:
            hv = bt[v] + a[v] * h
            rows[v] = hv
            h = hv[edge:edge + 1, :]
        return jnp.concatenate(rows, axis=0), h

    n_lat_chunks = n_lat // R
    h0 = jnp.zeros((1, RG_BW), F32)

    def fwd_chunk(c0, seg_lo, seg_hi, h):
        xc = conv_chunk(c0, seg_lo, seg_hi)
        a, bt = gates(xc, 0)
        hs, h = chunk_scan(a, bt, h, False)
        rs = pl.ds(pl.multiple_of(c0, 8), R)
        hf_ref[rs, :] = hs
        xc_ref[rs, :] = xc
        return h

    h = h0
    for c in range(n_ctx // R):
        h = fwd_chunk(n_lat + c * R, n_lat, T, h)
    h = lax.fori_loop(0, n_lat_chunks, lambda c, hh: fwd_chunk(c * R, 0, n_lat, hh), h)

    def bwd_chunk(c0, h):
        rs = pl.ds(pl.multiple_of(c0, 8), R)
        a, bt = gates(xc_ref[rs, :], 1)
        hs, h = chunk_scan(a, bt, h, True)
        o_ref[rs, :] = (gate_ref[rs, :].astype(F32) * (hf_ref[rs, :] + hs)).astype(o_ref.dtype)
        return h

    h = h0
    for c in range(n_ctx // R - 1, -1, -1):
        h = bwd_chunk(n_lat + c * R, h)
    lax.fori_loop(0, n_lat_chunks, lambda c, hh: bwd_chunk((n_lat_chunks - 1 - c) * R, hh), h)


def rglru(gate, xr, cw, cb, wg, bg, lam, n_lat, chunk=256):
    B, T, _ = xr.shape
    col = pl.BlockSpec((None, T, RG_BW), lambda b, n: (b, 0, n))
    return pl.pallas_call(
        functools.partial(_rglru_kernel, n_lat, chunk),
        grid=(B, RG_BLOCKS),
        in_specs=[
            col, col,
            pl.BlockSpec((RG_CONV, RG_BW), lambda b, n: (0, n)),
            pl.BlockSpec((1, RG_BW), lambda b, n: (0, n)),
            pl.BlockSpec((None, 2, RG_BW, 2 * RG_BW), lambda b, n: (n, 0, 0, 0)),
            pl.BlockSpec((None, 2, 1, 2 * RG_BW), lambda b, n: (n, 0, 0, 0)),
            pl.BlockSpec((None, 2, 1, RG_BW), lambda b, n: (n, 0, 0, 0)),
        ],
        out_specs=col,
        out_shape=jax.ShapeDtypeStruct((B, T, D_RNN), BF16),
        scratch_shapes=[pltpu.VMEM((T, RG_BW), F32), pltpu.VMEM((T, RG_BW), F32)],
        compiler_params=_cparams("parallel", "parallel"),
        name="rglru",
    )(gate, xr, cw, cb, wg, bg, lam)


def _hyena_filters(L, w1, b1, w2, b2, w3, freq):
    k = jnp.arange(L, dtype=F32)
    t = (k / max(L - 1, 1))[:, None]
    ang = (2.0 * math.pi / L) * k[:, None] * jnp.linspace(1e-4, HY_BANDS - 1, HY_BANDS, dtype=F32)[None, :]
    z = jnp.concatenate([t, jnp.cos(ang), -jnp.sin(ang)], axis=-1)
    mm = functools.partial(jnp.dot, precision=lax.Precision.HIGHEST)
    h = jnp.sin(freq * (mm(z, w1) + b1))
    h = jnp.sin(freq * (mm(h, w2) + b2))
    h = mm(h, w3).reshape(L, 2, D_HY)
    half = L // 2
    dist = jnp.abs(k - half) / max(half, 1)
    deltas = jnp.abs(jnp.linspace(math.log(HY_TARGET) / HY_SLOW_DECAY, math.log(HY_TARGET) / HY_FAST_DECAY,
                                  D_HY, dtype=F32))
    window = jnp.exp(-dist[:, None] * deltas[None, :])
    return jnp.transpose(h * window[:, None, :], (1, 0, 2))


def _swap_neg(x):
    return jnp.concatenate([x[:, HY_CB:], -x[:, :HY_CB]], axis=1)


def _cstack(x):
    return jnp.concatenate([x, _swap_neg(x)], axis=0)


def _cmul(h, w):
    hre = jnp.concatenate([h[:, :HY_CB], h[:, :HY_CB]], axis=1)
    him = jnp.concatenate([-h[:, HY_CB:], h[:, HY_CB:]], axis=1)
    wsw = jnp.concatenate([w[:, HY_CB:], w[:, :HY_CB]], axis=1)
    return hre * w + him * wsw


def _hy_short_kernel(u_ref, w_ref, b_ref, o_ref):
    L = u_ref.shape[1]
    row = lax.broadcasted_iota(jnp.int32, (L, 1), 0)
    w = w_ref[...]
    halves = []
    for b in range(2):
        u = u_ref[b].astype(F32)
        prev = jnp.where(row >= 1, pltpu.roll(u, 1, 0), 0.0)
        nxt = jnp.where(row < L - 1, pltpu.roll(u, L - 1, 0), 0.0)
        halves.append(prev * w[0:1] + u * w[1:2] + nxt * w[2:3] + b_ref[...])
    o_ref[...] = jnp.concatenate(halves, axis=1).astype(o_ref.dtype)


def hy_short(u_hy, short_w, short_b, row_blk, L):
    return pl.pallas_call(
        _hy_short_kernel,
        grid=(3, HY_NCB),
        in_specs=[
            pl.BlockSpec((2, L, HY_CB), lambda g, cb: (0, row_blk, g * HY_NCB + cb)),
            pl.BlockSpec((3, HY_CB), lambda g, cb: (0, g * HY_NCB + cb)),
            pl.BlockSpec((1, HY_CB), lambda g, cb: (0, g * HY_NCB + cb)),
        ],
        out_specs=pl.BlockSpec((None, None, L, 2 * HY_CB), lambda g, cb: (g, cb, 0, 0)),
        out_shape=jax.ShapeDtypeStruct((3, HY_NCB, L, 2 * HY_CB), BF16),
        compiler_params=_cparams("parallel", "parallel"),
        name="hy_short",
    )(u_hy, short_w, short_b)


def _hy_conv_kernel(z_ref, h_ref, xg_ref, b_ref, ma_ref, fb_ref, fc_ref, md_ref, o_ref, work, spec):
    n1 = z_ref.shape[0] // HY_N
    unroll = 8

    def put(rows, val):
        work[0, rows, :] = val[:, :HY_CB]
        work[1, rows, :] = val[:, HY_CB:]

    def get(rows):
        return jnp.concatenate([work[0, rows, :], work[1, rows, :]], axis=1).astype(BF16)

    def stage_a(src_ref):
        def step(i, carry):
            for g in range(unroll):
                j2 = i * unroll + g
                x = src_ref[pl.ds(pl.multiple_of(j2 * n1, 16), n1), :]
                a = jnp.dot(ma_ref[j2], _cstack(x), preferred_element_type=F32)
                put(pl.ds(j2, HY_N, stride=HY_PITCH), a)
            return carry

        lax.fori_loop(0, HY_N // unroll, step, 0)

    def mid(filtered):
        def step(i, carry):
            for g in range(unroll):
                k1 = i * unroll + g
                slab = pl.ds(pl.multiple_of(k1 * HY_PITCH, 8), HY_N)
                rows = pl.ds(pl.multiple_of(k1 * HY_N, 16), HY_N)
                w = jnp.dot(fb_ref[...], _cstack(get(slab)), preferred_element_type=F32)
                if filtered:
                    y = _cmul(spec[rows, :].astype(F32), w)
                    put(slab, jnp.dot(fc_ref[...], _cstack(y.astype(BF16)), preferred_element_type=F32))
                else:
                    spec[rows, :] = w.astype(spec.dtype)
            return carry

        lax.fori_loop(0, HY_N // unroll, step, 0)

    def stage_d(i, carry):
        for g in range(unroll):
            q2 = i * unroll + g
            y = jnp.dot(md_ref[q2], _cstack(get(pl.ds(q2, HY_N, stride=HY_PITCH))), preferred_element_type=F32)
            rows = pl.ds(pl.multiple_of(q2 * n1, 16), n1)
            gated = xg_ref[rows, :].astype(F32) * (y + b_ref[...] * z_ref[rows, :].astype(F32))
            o_ref[rows, :] = gated.astype(o_ref.dtype)
        return carry

    stage_a(h_ref)
    mid(False)
    stage_a(z_ref)
    mid(True)
    lax.fori_loop(0, HY_N // unroll, stage_d, 0)


def hy_conv(z_t, h_t, xg_t, bias_p, tabs):
    ncb, L, lanes = z_t.shape
    blk = pl.BlockSpec((None, L, lanes), lambda cb: (cb, 0, 0), pipeline_mode=pl.Buffered(1))
    consts = [tabs["ma"], tabs["fb"], tabs["fc"], tabs["md"]]
    return pl.pallas_call(
        _hy_conv_kernel,
        grid=(ncb,),
        in_specs=[blk, blk, blk, pl.BlockSpec((None, 1, lanes), lambda cb: (cb, 0, 0))]
        + [_const_spec(c.shape, True) for c in consts],
        out_specs=pl.BlockSpec((None, L, lanes), lambda cb: (cb, 0, 0)),
        out_shape=jax.ShapeDtypeStruct((ncb, L, lanes), BF16),
        scratch_shapes=[pltpu.VMEM((2, HY_N * HY_PITCH, HY_CB), F32), pltpu.VMEM((HY_N * HY_N, lanes), BF16)],
        compiler_params=_cparams("parallel"),
        name="hy_conv",
    )(z_t, h_t, xg_t, bias_p, *consts)


def _hy_ctx_kernel(s_ref, h_ref, b_ref, ff_ref, fi_ref, o_ref):
    z = s_ref[0].astype(F32)
    L = z.shape[0]
    ff = ff_ref[...]
    for o in range(2):
        hp = jnp.concatenate([h_ref[o], jnp.zeros((L, HY_CB), F32)], axis=1)
        spec = jnp.dot(ff, _cstack(hp.astype(BF16)), preferred_element_type=F32)
        w = jnp.dot(ff, _cstack(z.astype(BF16)), preferred_element_type=F32)
        y = jnp.dot(fi_ref[...], _cstack(_cmul(spec, w).astype(BF16)), preferred_element_type=F32)
        z = s_ref[o + 1].astype(F32) * (y + b_ref[o] * z)
    o_ref[0] = z[:, :HY_CB].astype(o_ref.dtype)
    o_ref[1] = z[:, HY_CB:].astype(o_ref.dtype)


def hy_ctx(s, h, bias_p, ff, fi):
    _, ncb, L, _ = s.shape
    return pl.pallas_call(
        _hy_ctx_kernel,
        grid=(ncb,),
        in_specs=[
            pl.BlockSpec((3, None, L, 2 * HY_CB), lambda cb: (0, cb, 0, 0)),
            pl.BlockSpec((2, L, HY_CB), lambda cb: (0, 0, cb)),
            pl.BlockSpec((2, None, 1, 2 * HY_CB), lambda cb: (0, cb, 0, 0)),
            _const_spec(ff.shape), _const_spec(fi.shape),
        ],
        out_specs=pl.BlockSpec((2, L, HY_CB), lambda cb: (0, 0, cb)),
        out_shape=jax.ShapeDtypeStruct((2, L, ncb * HY_CB), BF16),
        compiler_params=_cparams("parallel"),
        name="hy_ctx",
    )(s, h, bias_p, ff, fi)


def _hy_tables(L, L_ctx):
    n = 2 * L
    n1 = L // HY_N
    two_pi = 2.0 * math.pi
    ar = lambda m: jnp.arange(m, dtype=jnp.int32)
    cs = lambda m, period: (jnp.cos(m.astype(F32) * (two_pi / period)), jnp.sin(m.astype(F32) * (two_pi / period)))
    tc, ts = cs((ar(HY_N)[:, None] * ar(HY_N)[None, :]) % n, n)
    c1, s1 = cs((ar(HY_N)[:, None] * ar(n1)[None, :]) % HY_N, HY_N)
    c = c1[None] * tc[:, :, None] - s1[None] * ts[:, :, None]
    s = s1[None] * tc[:, :, None] + c1[None] * ts[:, :, None]
    ma = jnp.concatenate([c, s], axis=-1).astype(BF16)
    q1 = ar(n1) + n1 // 2
    c1, s1 = cs((q1[:, None] * ar(HY_N)[None, :]) % HY_N, HY_N)
    c = c1[None] * tc[:, None, :] - s1[None] * ts[:, None, :]
    s = s1[None] * tc[:, None, :] + c1[None] * ts[:, None, :]
    md = (jnp.concatenate([c, -s], axis=-1) / n).astype(BF16)
    m = (ar(HY_N)[:, None] * ar(HY_N)[None, :]) % HY_N
    c, s = cs(m, HY_N)
    fb = jnp.concatenate([c, s], axis=-1).astype(BF16)
    fc = jnp.concatenate([c, -s], axis=-1).astype(BF16)
    nc = 2 * L_ctx
    m = (ar(nc)[:, None] * ar(L_ctx)[None, :]) % nc
    c, s = cs(m, nc)
    ff = jnp.concatenate([c, s], axis=-1).astype(BF16)
    m = ((ar(L_ctx) + L_ctx // 2)[:, None] * ar(nc)[None, :]) % nc
    c, s = cs(m, nc)
    fi = (jnp.concatenate([c, -s], axis=-1) / nc).astype(BF16)
    return {"ma": ma, "md": md, "fb": fb, "fc": fc, "ff": ff, "fi": fi}


def hyena(u_hy, short_w, short_b, filt, bias, tabs, n_lat, n_ctx):
    bias_p = jnp.concatenate([bias.reshape(2, HY_NCB, 1, HY_CB)] * 2, axis=-1)
    n1 = n_lat // HY_N
    to_slabs = lambda t: jnp.swapaxes(t.reshape(HY_NCB, n1, HY_N, -1), 1, 2).reshape(HY_NCB, n_lat, -1)
    h_lat = _hyena_filters(n_lat, *filt)
    h_lat = jnp.transpose(h_lat.reshape(2, n1, HY_N, HY_NCB, HY_CB), (0, 3, 2, 1, 4)).reshape(2, HY_NCB, n_lat, HY_CB)
    h_lat = jnp.concatenate([h_lat, jnp.zeros_like(h_lat)], axis=-1).astype(BF16)
    s = hy_short(u_hy, short_w, short_b, 0, n_lat)
    z = to_slabs(s[0])
    for o in range(2):
        z = hy_conv(z, h_lat[o], to_slabs(s[o + 1]), bias_p[o], tabs)
    z = jnp.transpose(z.reshape(HY_NCB, HY_N, n1, 2, HY_CB), (3, 2, 1, 0, 4)).reshape(2, n_lat, D_HY)
    s_c = hy_short(u_hy, short_w, short_b, n_lat // n_ctx, n_ctx)
    z_c = hy_ctx(s_c, _hyena_filters(n_ctx, *filt), bias_p, tabs["ff"], tabs["fi"])
    return jnp.concatenate([z, z_c], axis=1)


def _rope_tables(n_lat, n_ctx):
    rows = n_lat // GRID_W
    inv_freq = ROPE_BASE ** (-jnp.arange(0, ROPE_AXIS, 2, dtype=F32) / ROPE_AXIS)
    ang_r = jnp.arange(rows, dtype=F32)[:, None] * inv_freq[None, :]
    ang_c = jnp.arange(GRID_W, dtype=F32)[:, None] * inv_freq[None, :]
    nf = ROPE_AXIS // 2
    per_row = lambda t: jnp.broadcast_to(t[:, None, :], (rows, GRID_W, nf)).reshape(n_lat, nf)
    per_col = lambda t: jnp.broadcast_to(t[None, :, :], (rows, GRID_W, nf)).reshape(n_lat, nf)
    cos = jnp.concatenate([per_row(jnp.cos(ang_r)), per_col(jnp.cos(ang_c))], axis=-1)
    sin = jnp.concatenate([per_row(jnp.sin(ang_r)), per_col(jnp.sin(ang_c))], axis=-1)
    cos = jnp.concatenate([cos, jnp.ones((n_ctx, ROPE_AXIS), F32)], axis=0)
    sin = jnp.concatenate([sin, jnp.zeros((n_ctx, ROPE_AXIS), F32)], axis=0)
    T = n_lat + n_ctx
    z8 = jnp.zeros((T, ROPE_AXIS // 2), F32)
    cos32 = jnp.concatenate([cos[:, :8], cos[:, :8], cos[:, 8:], cos[:, 8:]], axis=-1)
    first = jnp.concatenate([-sin[:, :8], z8, -sin[:, 8:], z8], axis=-1)
    second = jnp.concatenate([z8, sin[:, :8], z8, sin[:, 8:]], axis=-1)
    pad = lambda t, fill: jnp.concatenate(
        [jnp.full((T, MLA_NOPE), fill, F32), t, jnp.zeros((T, HEAD_LANES - MLA_NOPE - MLA_ROPE), F32)], axis=-1)
    lane = jnp.arange(HEAD_LANES)
    seg = jnp.where(lane < MLA_NOPE, 0, jnp.where(lane < MLA_NOPE + MLA_ROPE, 1, 2))
    msq = (seg[:, None] == seg[None, :]) & (seg[:, None] < 2)
    msq = jnp.where(msq, jnp.where(seg[:, None] == 0, 1.0 / MLA_NOPE, 1.0 / MLA_ROPE), 0.0).astype(BF16)
    return {"ct": pad(cos32, 1.0), "s1": pad(first, 0.0), "s2": pad(second, 0.0), "msq": msq}


def _head_gain(g, scale):
    return jnp.concatenate([g * scale, jnp.zeros((HEAD_LANES - g.shape[0],), F32)])[None, :]


def _even_weights(j, ev_w_in, mla_q_norm_g, mla_w_uq, mla_kv_norm_g, mla_w_ukv, mla_qk_g_q, mla_qk_g_k):
    i_q = 3 * D_HY
    i_kv = i_q + MLA_Q_RANK
    i_kr = i_kv + MLA_KV_RANK
    w_in = ev_w_in[j]
    d = w_in.shape[0]
    kr_group = jnp.concatenate([jnp.zeros((d, MLA_NOPE), F32), w_in[:, i_kr:],
                                jnp.zeros((d, HEAD_LANES - MLA_NOPE - MLA_ROPE), F32)], axis=-1)
    w_in_p = jnp.concatenate([w_in[:, :i_kr], kr_group], axis=-1).astype(BF16)
    hq = MLA_NOPE + MLA_ROPE
    w_uq = mla_w_uq[j].reshape(MLA_Q_RANK, MLA_HEADS, hq)
    w_uq = jnp.pad(w_uq, ((0, 0), (0, 0), (0, HEAD_LANES - hq))).reshape(MLA_Q_RANK, -1).astype(BF16)
    w_ukv = mla_w_ukv[j].reshape(MLA_KV_RANK, MLA_HEADS, MLA_NOPE + MLA_V)
    w_uk = jnp.pad(w_ukv[:, :, :MLA_NOPE], ((0, 0), (0, 0), (0, HEAD_LANES - MLA_NOPE)))
    w_uk = w_uk.reshape(MLA_KV_RANK, -1).astype(BF16)
    w_uv = w_ukv[:, :, MLA_NOPE:].reshape(MLA_KV_RANK, -1).astype(BF16)
    return {
        "w_in": w_in_p, "q_norm_g": mla_q_norm_g[j][None, :], "kv_norm_g": mla_kv_norm_g[j][None, :],
        "w_uq": w_uq, "w_uk": w_uk, "w_uv": w_uv,
        "gq": _head_gain(mla_qk_g_q[j], MLA_SCALE * math.log2(math.e)), "gk": _head_gain(mla_qk_g_k[j], 1.0),
    }


def _rg_gate_weights(j, rg_w_a, rg_b_a, rg_w_x, rg_b_x, rg_lambda):
    wg = 0.5 * jnp.concatenate([rg_w_a[j], rg_w_x[j]], axis=-1)
    wg = jnp.transpose(wg, (1, 0, 2, 3)).astype(BF16)
    blk = lambda t: jnp.transpose(t.reshape(2, RG_BLOCKS, RG_BW), (1, 0, 2))
    bg = 0.5 * jnp.concatenate([blk(rg_b_a[j]), blk(rg_b_x[j])], axis=-1)[:, :, None, :]
    lam = blk(rg_lambda[j])[:, :, None, :]
    return wg, bg, lam


def kernel(x, c, ctx, c_ctx, ada_w, ada_b, norm1_g, norm2_g, ffn_w_in, ffn_w_out, ev_w_in, hy_short_w, hy_short_b, hy_w1, hy_b1, hy_w2, hy_b2, hy_w3, hy_freq, hy_bias, mla_q_norm_g, mla_w_uq, mla_kv_norm_g, mla_w_ukv, mla_qk_g_q, mla_qk_g_k, ev_w_out, rg_w_in, rg_conv_w, rg_conv_b, rg_w_a, rg_b_a, rg_w_x, rg_b_x, rg_lambda, rg_w_out):
    B, n_lat, D = x.shape
    n_ctx = ctx.shape[1]
    T = n_lat + n_ctx
    bm = 768
    assert T % bm == 0 and n_lat % 512 == 0 and n_ctx == 256

    cond8 = jnp.concatenate([c, c_ctx[None, :], jnp.zeros((8 - B - 1, D), F32)], axis=0)
    mods = ada_modulation(cond8, ada_w, ada_b)
    tabs = _rope_tables(n_lat, n_ctx)
    hy_tabs = _hy_tables(n_lat, n_ctx)
    x_all = jnp.concatenate([x, ctx], axis=1)

    for i in range(DEPTH):
        j = i // 2
        last = i == DEPTH - 1
        m = mods[i]
        g1 = norm1_g[i][None, :]
        g2 = norm2_g[i][None, :]
        if i % 2 == 0:
            w = _even_weights(j, ev_w_in, mla_q_norm_g, mla_w_uq, mla_kv_norm_g, mla_w_ukv, mla_qk_g_q, mla_qk_g_k)
            u_hy, q, k, v = even_pre(x_all, m, g1, w, tabs, n_lat, bm)
            att = mla_attend(q, k, v, n_lat)
            filt = (hy_w1[j], hy_b1[j], hy_w2[j], hy_b2[j], hy_w3[j], hy_freq[j])
            hy = hyena(u_hy, hy_short_w[j], hy_short_b[j][None, :], filt, hy_bias[j], hy_tabs, n_lat, n_ctx)
            w_out = ev_w_out[j].astype(BF16)
            mixes = [(hy, w_out[:D_HY]), (att, w_out[D_HY:])]
        else:
            gate, xr = odd_pre(x_all, m, g1, rg_w_in[j].astype(BF16), n_lat, bm)
            wg, bg, lam = _rg_gate_weights(j, rg_w_a, rg_b_a, rg_w_x, rg_b_x, rg_lambda)
            y = rglru(gate, xr, rg_conv_w[j], rg_conv_b[j][None, :], wg, bg, lam, n_lat)
            mixes = [(y, rg_w_out[j].astype(BF16))]
        wg_f = ffn_w_in[i][:, :D_FF].astype(BF16)
        wu_f = ffn_w_in[i][:, D_FF:].astype(BF16)
        wd_f = ffn_w_out[i].astype(BF16)
        if last:
            x_all = post_mixer(x_all, m, g2, mixes, wg_f, wu_f, wd_f, n_lat, n_lat, 512)
        else:
            x_all = post_mixer(x_all, m, g2, mixes, wg_f, wu_f, wd_f, n_lat, T, bm)
    return x_all
```

```python
import functools
import math

import jax
import jax.numpy as jnp
from jax import lax
from jax.experimental import pallas as pl
from jax.experimental.pallas import tpu as pltpu

F32 = jnp.float32
BF16 = jnp.bfloat16
U32 = jnp.uint32

D_MODEL = 1024
DEPTH = 4
GRID_W = 64
D_FF = 2816
NORM_EPS = 1e-6

D_HY = 512
HY_CB = 128
HY_NCB = D_HY // HY_CB
HY_N = 128
HY_PITCH = HY_N + 8
HY_BANDS = 16
HY_FAST_DECAY = 0.3
HY_SLOW_DECAY = 1.5
HY_TARGET = 1e-2

MLA_HEADS = 8
MLA_NOPE = 64
MLA_ROPE = 32
MLA_V = 64
MLA_VA = MLA_V + 16
MLA_Q_RANK = 384
MLA_KV_RANK = 256
MLA_SCALE = (MLA_NOPE + MLA_ROPE) ** -0.5
ROPE_AXIS = 16
ROPE_BASE = 10000.0
HEAD_LANES = 128

D_RNN = 1280
RG_BLOCKS = 10
RG_BW = 128
RG_CONV = 4
RG_C = 8.0

EVEN_IN_PAD = 3 * D_HY + MLA_Q_RANK + MLA_KV_RANK + HEAD_LANES

VMEM_LIMIT = 56 * 1024 * 1024


def _cparams(*sem):
    return pltpu.CompilerParams(dimension_semantics=sem, vmem_limit_bytes=VMEM_LIMIT)


def _const_spec(shape, single_buffer=False):
    nd = len(shape)
    kw = {"pipeline_mode": pl.Buffered(1)} if single_buffer else {}
    return pl.BlockSpec(shape, lambda *_: (0,) * nd, **kw)


def _ada_kernel(s_ref, w_ref, b_ref, o_ref):
    s = s_ref[...]
    s = s * jax.nn.sigmoid(s)
    o_ref[...] = jnp.dot(s, w_ref[...], preferred_element_type=F32) + b_ref[...]


def ada_modulation(cond8, ada_w, ada_b):
    depth, d, n = ada_w.shape
    tn = 1536
    return pl.pallas_call(
        _ada_kernel,
        grid=(depth, n // tn),
        in_specs=[
            pl.BlockSpec((8, d), lambda i, j: (0, 0)),
            pl.BlockSpec((None, d, tn), lambda i, j: (i, 0, j)),
            pl.BlockSpec((None, 1, tn), lambda i, j: (i, 0, j)),
        ],
        out_specs=pl.BlockSpec((None, 8, tn), lambda i, j: (i, 0, j)),
        out_shape=jax.ShapeDtypeStruct((depth, 8, n), F32),
        compiler_params=_cparams("parallel", "parallel"),
        name="ada_modulation",
    )(cond8, ada_w, ada_b.reshape(depth, 1, n))


def _mod_rows(m_ref, idx, b, is_ctx):
    d = D_MODEL
    lat = m_ref[pl.ds(b, 1), pl.ds(idx * d, d)]
    ctx = m_ref[pl.ds(2, 1), pl.ds(idx * d, d)]
    return jnp.where(is_ctx, ctx, lat)


def _rms(x, g):
    ms = jnp.mean(x * x, axis=-1, keepdims=True)
    return x * lax.rsqrt(ms + NORM_EPS) * g


def _ctx_rows(i, bm, n_lat):
    row = i * bm + lax.broadcasted_iota(jnp.int32, (bm, 1), 0)
    return row >= n_lat


def _even_pre_kernel(n_lat, x_ref, m_ref, g1_ref, w_in_ref, qg_ref, kvg_ref, w_uq_ref, w_uk_ref,
                     w_uv_ref, gq_ref, gk_ref, msq_ref, ct_ref, s1_ref, s2_ref,
                     hy_ref, q_ref, k_ref, v_ref):
    b = pl.program_id(0)
    i = pl.program_id(1)
    bm = x_ref.shape[0]
    is_ctx = _ctx_rows(i, bm, n_lat)
    x = x_ref[...]
    h = _rms(x, g1_ref[...])
    h = h * (1.0 + _mod_rows(m_ref, 1, b, is_ctx)) + _mod_rows(m_ref, 0, b, is_ctx)
    u = jnp.dot(h.astype(BF16), w_in_ref[...], preferred_element_type=F32)
    i_q = 3 * D_HY
    i_kv = i_q + MLA_Q_RANK
    i_kr = i_kv + MLA_KV_RANK
    hy_ref[...] = u[:, :i_q].astype(BF16)

    ct = ct_ref[...]
    s1 = s1_ref[...]
    s2 = s2_ref[...]
    msq = msq_ref[...]

    def head_norm(t, g):
        ms = jnp.dot((t * t).astype(BF16), msq, preferred_element_type=F32)
        return t * lax.rsqrt(ms + NORM_EPS) * g

    def rope(t):
        return (t * ct + pltpu.roll(t, HEAD_LANES - ROPE_AXIS // 2, 1) * s1
                + pltpu.roll(t, ROPE_AXIS // 2, 1) * s2)

    qn = _rms(u[:, i_q:i_kv], qg_ref[...]).astype(BF16)
    kvn = _rms(u[:, i_kv:i_kr], kvg_ref[...]).astype(BF16)
    gq = gq_ref[...]
    gk = gk_ref[...]
    k_rope = rope(head_norm(u[:, i_kr:], gk))
    for hd in range(MLA_HEADS):
        sl = slice(hd * HEAD_LANES, (hd + 1) * HEAD_LANES)
        qh = jnp.dot(qn, w_uq_ref[:, sl], preferred_element_type=F32)
        q_ref[hd] = rope(head_norm(qh, gq)).T.astype(BF16)
        kh = jnp.dot(kvn, w_uk_ref[:, sl], preferred_element_type=F32)
        k_ref[hd] = (head_norm(kh, gk) + k_rope).astype(BF16)
    v_t = jnp.dot(kvn, w_uv_ref[...], preferred_element_type=F32).T
    ones = jnp.ones((MLA_VA - MLA_V, bm), F32)
    for hd in range(MLA_HEADS):
        v_ref[hd] = jnp.concatenate([v_t[hd * MLA_V:(hd + 1) * MLA_V, :], ones], axis=0).astype(BF16)


def even_pre(x_all, m, g1, w, tabs, n_lat, bm):
    B, T, D = x_all.shape
    H = MLA_HEADS
    row = lambda n: pl.BlockSpec((None, bm, n), lambda b, i: (b, i, 0))
    tab = pl.BlockSpec((bm, HEAD_LANES), lambda b, i: (i, 0))
    headed = lambda nh: pl.BlockSpec((None, nh, bm, HEAD_LANES), lambda b, i: (b, 0, i, 0))
    consts = [m, g1, w["w_in"], w["q_norm_g"], w["kv_norm_g"], w["w_uq"], w["w_uk"], w["w_uv"],
              w["gq"], w["gk"], tabs["msq"]]
    return pl.pallas_call(
        functools.partial(_even_pre_kernel, n_lat),
        grid=(B, T // bm),
        in_specs=[row(D)] + [_const_spec(c.shape) for c in consts] + [tab, tab, tab],
        out_specs=[
            row(3 * D_HY),
            pl.BlockSpec((None, H, HEAD_LANES, bm), lambda b, i: (b, 0, 0, i)),
            headed(H),
            pl.BlockSpec((None, H, None, MLA_VA, bm), lambda b, i: (b, 0, i, 0, 0)),
        ],
        out_shape=[
            jax.ShapeDtypeStruct((B, T, 3 * D_HY), BF16),
            jax.ShapeDtypeStruct((B, H, HEAD_LANES, T), BF16),
            jax.ShapeDtypeStruct((B, H, T, HEAD_LANES), BF16),
            jax.ShapeDtypeStruct((B, H, T // bm, MLA_VA, bm), BF16),
        ],
        compiler_params=_cparams("parallel", "parallel"),
        name="even_pre",
    )(x_all, *consts, tabs["ct"], tabs["s1"], tabs["s2"])


def _attn_kernel(k_ref, qt_ref, vt_ref, o_ref, s_ref, mx_ref):
    n_chunks = k_ref.shape[1]
    tq = qt_ref.shape[2]
    assert n_chunks % 2 == 1

    def scores(c, slot):
        for hh in range(2):
            s = jnp.dot(k_ref[hh, c], qt_ref[hh], preferred_element_type=F32)
            s_ref[slot, hh] = s
            mx_ref[slot, hh] = jnp.max(s, axis=0, keepdims=True)

    def absorb(c, slot, stats):
        out = []
        for hh in range(2):
            m, acc = stats[hh]
            m_new = jnp.maximum(m, mx_ref[slot, hh])
            p = jnp.exp2(s_ref[slot, hh] - m_new)
            acc = jnp.exp2(m - m_new) * acc + jnp.dot(vt_ref[hh, c], p.astype(BF16), preferred_element_type=F32)
            out.append((m_new, acc))
        return tuple(out)

    def body(i, stats):
        c = 2 * i
        scores(c + 1, 1)
        stats = absorb(c, 0, stats)
        scores(c + 2, 0)
        return absorb(c + 1, 1, stats)

    init = tuple((jnp.full((1, tq), -jnp.inf, F32), jnp.zeros((MLA_VA, tq), F32)) for _ in range(2))
    scores(0, 0)
    stats = lax.fori_loop(0, n_chunks // 2, body, init)
    res = absorb(n_chunks - 1, 0, stats)
    out_t = jnp.concatenate([acc[:MLA_V] / acc[MLA_V:MLA_V + 1] for _, acc in res], axis=0)
    o_ref[...] = out_t.T.astype(o_ref.dtype)


def attention(k_chunks, q_t, v_t, q_blk0, n_q, tq):
    B, H, n_chunks, tk, _ = k_chunks.shape
    return pl.pallas_call(
        _attn_kernel,
        grid=(B, H // 2, n_q // tq),
        in_specs=[
            pl.BlockSpec((None, 2, n_chunks, tk, HEAD_LANES), lambda b, h, i: (b, h, 0, 0, 0)),
            pl.BlockSpec((None, 2, HEAD_LANES, tq), lambda b, h, i: (b, h, 0, q_blk0 + i)),
            pl.BlockSpec((None, 2, n_chunks, MLA_VA, tk), lambda b, h, i: (b, h, 0, 0, 0)),
        ],
        out_specs=pl.BlockSpec((None, tq, 2 * MLA_V), lambda b, h, i: (b, i, h)),
        out_shape=jax.ShapeDtypeStruct((B, n_q, H * MLA_V), BF16),
        scratch_shapes=[pltpu.VMEM((2, 2, tk, tq), F32), pltpu.VMEM((2, 2, 1, tq), F32)],
        compiler_params=_cparams("parallel", "parallel", "parallel"),
        name="mla_attention",
    )(k_chunks, q_t, v_t)


def mla_attend(q_t, k, v_t, n_lat, tq=1024):
    B, H, T, _ = k.shape
    n_chunks, tk = v_t.shape[2], v_t.shape[4]
    n_ctx = T - n_lat
    att_l = attention(k.reshape(B, H, n_chunks, tk, HEAD_LANES), q_t, v_t, 0, n_lat, tq)
    k_c = k[:, :, n_lat:].reshape(B, H, 1, n_ctx, HEAD_LANES)
    v_c = v_t[:, :, n_chunks - 1, :, tk - n_ctx:].reshape(B, H, 1, MLA_VA, n_ctx)
    att_c = attention(k_c, q_t, v_c, n_lat // n_ctx, n_ctx, n_ctx)
    return jnp.concatenate([att_l, att_c], axis=1)


def _post_kernel(n_mix, n_lat, tf, *refs):
    x_ref, m_ref, g2_ref = refs[:3]
    mix = refs[3:3 + 2 * n_mix]
    wg_ref, wu_ref, wd_ref, o_ref = refs[3 + 2 * n_mix:]
    b = pl.program_id(0)
    i = pl.program_id(1)
    bm = x_ref.shape[0]
    is_ctx = _ctx_rows(i, bm, n_lat)
    y = None
    for j in range(n_mix):
        t = jnp.dot(mix[2 * j][...], mix[2 * j + 1][...], preferred_element_type=F32)
        y = t if y is None else y + t
    x1 = x_ref[...] + _mod_rows(m_ref, 2, b, is_ctx) * y
    h = _rms(x1, g2_ref[...])
    h = (h * (1.0 + _mod_rows(m_ref, 4, b, is_ctx)) + _mod_rows(m_ref, 3, b, is_ctx)).astype(BF16)
    acc = None
    for f0 in range(0, D_FF, tf):
        g = jnp.dot(h, wg_ref[:, f0:f0 + tf], preferred_element_type=F32)
        u = jnp.dot(h, wu_ref[:, f0:f0 + tf], preferred_element_type=F32)
        a = (g * jax.nn.sigmoid(g) * u).astype(BF16)
        t = jnp.dot(a, wd_ref[f0:f0 + tf, :], preferred_element_type=F32)
        acc = t if acc is None else acc + t
    o_ref[...] = x1 + _mod_rows(m_ref, 5, b, is_ctx) * acc


def post_mixer(x_all, m, g2, mixes, wg, wu, wd, n_lat, n_rows, bm, tf=1408):
    B, T, D = x_all.shape
    row = lambda n: pl.BlockSpec((None, bm, n), lambda b, i: (b, i, 0))
    mix_args, mix_specs = [], []
    for a, wmat in mixes:
        mix_args += [a, wmat]
        mix_specs += [row(a.shape[-1]), _const_spec(wmat.shape, True)]
    return pl.pallas_call(
        functools.partial(_post_kernel, len(mixes), n_lat, tf),
        grid=(B, n_rows // bm),
        in_specs=[row(D), _const_spec(m.shape), _const_spec(g2.shape)] + mix_specs
        + [_const_spec(wg.shape, True), _const_spec(wu.shape, True), _const_spec(wd.shape, True)],
        out_specs=row(D),
        out_shape=jax.ShapeDtypeStruct((B, n_rows, D), F32),
        compiler_params=_cparams("parallel", "parallel"),
        name="post_mixer",
    )(x_all, m, g2, *mix_args, wg, wu, wd)


def _odd_pre_kernel(n_lat, x_ref, m_ref, g1_ref, w_ref, gate_ref, xr_ref):
    b = pl.program_id(0)
    i = pl.program_id(1)
    bm = x_ref.shape[0]
    is_ctx = _ctx_rows(i, bm, n_lat)
    h = _rms(x_ref[...], g1_ref[...])
    h = h * (1.0 + _mod_rows(m_ref, 1, b, is_ctx)) + _mod_rows(m_ref, 0, b, is_ctx)
    u = jnp.dot(h.astype(BF16), w_ref[...], preferred_element_type=F32)
    gate_ref[...] = jax.nn.gelu(u[:, :D_RNN]).astype(BF16)
    xr_ref[...] = u[:, D_RNN:]


def odd_pre(x_all, m, g1, w_in, n_lat, bm):
    B, T, D = x_all.shape
    row = lambda n: pl.BlockSpec((None, bm, n), lambda b, i: (b, i, 0))
    return pl.pallas_call(
        functools.partial(_odd_pre_kernel, n_lat),
        grid=(B, T // bm),
        in_specs=[row(D), _const_spec(m.shape), _const_spec(g1.shape), _const_spec(w_in.shape)],
        out_specs=[row(D_RNN), row(D_RNN)],
        out_shape=[jax.ShapeDtypeStruct((B, T, D_RNN), BF16), jax.ShapeDtypeStruct((B, T, D_RNN), F32)],
        compiler_params=_cparams("parallel", "parallel"),
        name="odd_pre",
    )(x_all, m, g1, w_in)


def _rglru_kernel(n_lat, chunk, gate_ref, xr_ref, cw_ref, cb_ref, wg_ref, bg_ref, lam_ref, o_ref, hf_ref, xc_ref):
    T = xr_ref.shape[0]
    n_ctx = T - n_lat
    R = chunk
    nv = R // 8
    cw = cw_ref[...]
    cb = cb_ref[...]
    sub = lax.broadcasted_iota(jnp.int32, (nv, 8, RG_BW), 1)

    def conv_chunk(c0, seg_lo, seg_hi):
        lo = jnp.maximum(c0 - 8, 0)
        hi = jnp.minimum(c0 + R, T - 8)
        win = jnp.concatenate([xr_ref[pl.ds(pl.multiple_of(lo, 8), 8), :],
                               xr_ref[pl.ds(pl.multiple_of(c0, 8), R), :],
                               xr_ref[pl.ds(pl.multiple_of(hi, 8), 8), :]], axis=0)
        g = c0 - 8 + lax.broadcasted_iota(jnp.int32, (R + 16, 1), 0)
        win = jnp.where((g >= seg_lo) & (g < seg_hi), win, 0.0)
        y = cb
        for kk in range(RG_CONV):
            off = 8 + kk - RG_CONV // 2
            y = y + win[off:off + R, :] * cw[kk:kk + 1, :]
        return y

    def gates(xc, d):
        t = jnp.tanh(jnp.dot(xc.astype(BF16), wg_ref[d], preferred_element_type=F32) + bg_ref[d])
        ig = 0.5 * t[:, RG_BW:] + 0.5
        c_half = (-0.5 * RG_C) * jax.nn.softplus(-lam_ref[d])
        log_a = t[:, :RG_BW] * c_half + c_half
        a = jnp.exp(log_a)
        bt = jnp.sqrt(-jnp.tanh(log_a) * (a * a + 1.0)) * (ig * xc)
        return a, bt

    def local_scan(a, bt, reverse):
        a = a.reshape(nv, 8, RG_BW)
        bt = bt.reshape(nv, 8, RG_BW)
        for d in (1, 2, 4):
            if reverse:
                keep = sub < 8 - d
                sh = 8 - d
            else:
                keep = sub >= d
                sh = d
            a_s = jnp.where(keep, pltpu.roll(a, sh, 1), 1.0)
            b_s = jnp.where(keep, pltpu.roll(bt, sh, 1), 0.0)
            bt = a * b_s + bt
            a = a * a_s
        return a, bt

    def chunk_scan(a, bt, h, reverse):
        a, bt = local_scan(a, bt, reverse)
        rows = [None] * nv
        order = range(nv - 1, -1, -1) if reverse else range(nv)
        edge = 0 if reverse else 7
        for v in order:
            hv = bt[v] + a[v] * h
            rows[v] = hv
            h = hv[edge:edge + 1, :]
        return jnp.concatenate(rows, axis=0), h

    n_lat_chunks = n_lat // R
    h0 = jnp.zeros((1, RG_BW), F32)

    def fwd_chunk(c0, seg_lo, seg_hi, h):
        xc = conv_chunk(c0, seg_lo, seg_hi)
        a, bt = gates(xc, 0)
        hs, h = chunk_scan(a, bt, h, False)
        rs = pl.ds(pl.multiple_of(c0, 8), R)
        hf_ref[rs, :] = hs
        xc_ref[rs, :] = xc
        return h

    h = h0
    for c in range(n_ctx // R):
        h = fwd_chunk(n_lat + c * R, n_lat, T, h)
    h = lax.fori_loop(0, n_lat_chunks, lambda c, hh: fwd_chunk(c * R, 0, n_lat, hh), h, unroll=2)

    def bwd_chunk(c0, h):
        rs = pl.ds(pl.multiple_of(c0, 8), R)
        a, bt = gates(xc_ref[rs, :], 1)
        hs, h = chunk_scan(a, bt, h, True)
        o_ref[rs, :] = (gate_ref[rs, :].astype(F32) * (hf_ref[rs, :] + hs)).astype(o_ref.dtype)
        return h

    h = h0
    for c in range(n_ctx // R - 1, -1, -1):
        h = bwd_chunk(n_lat + c * R, h)
    lax.fori_loop(0, n_lat_chunks, lambda c, hh: bwd_chunk((n_lat_chunks - 1 - c) * R, hh), h, unroll=2)


def rglru(gate, xr, cw, cb, wg, bg, lam, n_lat, chunk=256):
    B, T, _ = xr.shape
    col = pl.BlockSpec((None, T, RG_BW), lambda b, n: (b, 0, n))
    return pl.pallas_call(
        functools.partial(_rglru_kernel, n_lat, chunk),
        grid=(B, RG_BLOCKS),
        in_specs=[
            col, col,
            pl.BlockSpec((RG_CONV, RG_BW), lambda b, n: (0, n)),
            pl.BlockSpec((1, RG_BW), lambda b, n: (0, n)),
            pl.BlockSpec((None, 2, RG_BW, 2 * RG_BW), lambda b, n: (n, 0, 0, 0)),
            pl.BlockSpec((None, 2, 1, 2 * RG_BW), lambda b, n: (n, 0, 0, 0)),
            pl.BlockSpec((None, 2, 1, RG_BW), lambda b, n: (n, 0, 0, 0)),
        ],
        out_specs=col,
        out_shape=jax.ShapeDtypeStruct((B, T, D_RNN), BF16),
        scratch_shapes=[pltpu.VMEM((T, RG_BW), F32), pltpu.VMEM((T, RG_BW), F32)],
        compiler_params=_cparams("parallel", "parallel"),
        name="rglru",
    )(gate, xr, cw, cb, wg, bg, lam)


def _hyena_filters(L, w1, b1, w2, b2, w3, freq):
    k = jnp.arange(L, dtype=F32)
    t = (k / max(L - 1, 1))[:, None]
    ang = (2.0 * math.pi / L) * k[:, None] * jnp.linspace(1e-4, HY_BANDS - 1, HY_BANDS, dtype=F32)[None, :]
    z = jnp.concatenate([t, jnp.cos(ang), -jnp.sin(ang)], axis=-1)
    mm = functools.partial(jnp.dot, precision=lax.Precision.HIGHEST)
    h = jnp.sin(freq * (mm(z, w1) + b1))
    h = jnp.sin(freq * (mm(h, w2) + b2))
    h = mm(h, w3).reshape(L, 2, D_HY)
    half = L // 2
    dist = jnp.abs(k - half) / max(half, 1)
    deltas = jnp.abs(jnp.linspace(math.log(HY_TARGET) / HY_SLOW_DECAY, math.log(HY_TARGET) / HY_FAST_DECAY,
                                  D_HY, dtype=F32))
    window = jnp.exp(-dist[:, None] * deltas[None, :])
    return jnp.transpose(h * window[:, None, :], (1, 0, 2))


def _swap_neg(x):
    return jnp.concatenate([x[:, HY_CB:], -x[:, :HY_CB]], axis=1)


def _cstack(x):
    return jnp.concatenate([x, _swap_neg(x)], axis=0)


def _cmul(h, w):
    hre = jnp.concatenate([h[:, :HY_CB], h[:, :HY_CB]], axis=1)
    him = jnp.concatenate([-h[:, HY_CB:], h[:, HY_CB:]], axis=1)
    wsw = jnp.concatenate([w[:, HY_CB:], w[:, :HY_CB]], axis=1)
    return hre * w + him * wsw


def _hy_short_kernel(u_ref, w_ref, b_ref, o_ref):
    L = u_ref.shape[1]
    row = lax.broadcasted_iota(jnp.int32, (L, 1), 0)
    w = w_ref[...]
    halves = []
    for b in range(2):
        u = u_ref[b].astype(F32)
        prev = jnp.where(row >= 1, pltpu.roll(u, 1, 0), 0.0)
        nxt = jnp.where(row < L - 1, pltpu.roll(u, L - 1, 0), 0.0)
        halves.append(prev * w[0:1] + u * w[1:2] + nxt * w[2:3] + b_ref[...])
    o_ref[...] = jnp.concatenate(halves, axis=1).astype(o_ref.dtype)


def hy_short(u_hy, short_w, short_b, row_blk, L):
    return pl.pallas_call(
        _hy_short_kernel,
        grid=(3, HY_NCB),
        in_specs=[
            pl.BlockSpec((2, L, HY_CB), lambda g, cb: (0, row_blk, g * HY_NCB + cb)),
            pl.BlockSpec((3, HY_CB), lambda g, cb: (0, g * HY_NCB + cb)),
            pl.BlockSpec((1, HY_CB), lambda g, cb: (0, g * HY_NCB + cb)),
        ],
        out_specs=pl.BlockSpec((None, None, L, 2 * HY_CB), lambda g, cb: (g, cb, 0, 0)),
        out_shape=jax.ShapeDtypeStruct((3, HY_NCB, L, 2 * HY_CB), BF16),
        compiler_params=_cparams("parallel", "parallel"),
        name="hy_short",
    )(u_hy, short_w, short_b)


def _hy_conv_kernel(z_ref, h_ref, ma_ref, fb_ref, fc_ref, md_ref, o_ref, work, spec):
    n1 = z_ref.shape[0] // HY_N
    unroll = 8

    def put(rows, val):
        work[0, rows, :] = val[:, :HY_CB]
        work[1, rows, :] = val[:, HY_CB:]

    def get(rows):
        return jnp.concatenate([work[0, rows, :], work[1, rows, :]], axis=1).astype(BF16)

    def stage_a(src_ref):
        def step(i, carry):
            for g in range(unroll):
                j2 = i * unroll + g
                x = src_ref[pl.ds(pl.multiple_of(j2 * n1, 16), n1), :]
                a = jnp.dot(ma_ref[j2], _cstack(x), preferred_element_type=F32)
                put(pl.ds(j2, HY_N, stride=HY_PITCH), a)
            return carry

        lax.fori_loop(0, HY_N // unroll, step, 0)

    def mid(filtered):
        def step(i, carry):
            for g in range(unroll):
                k1 = i * unroll + g
                slab = pl.ds(pl.multiple_of(k1 * HY_PITCH, 8), HY_N)
                rows = pl.ds(pl.multiple_of(k1 * HY_N, 16), HY_N)
                w = jnp.dot(fb_ref[...], _cstack(get(slab)), preferred_element_type=F32)
                if filtered:
                    y = _cmul(spec[rows, :].astype(F32), w)
                    put(slab, jnp.dot(fc_ref[...], _cstack(y.astype(BF16)), preferred_element_type=F32))
                else:
                    spec[rows, :] = w.astype(spec.dtype)
            return carry

        lax.fori_loop(0, HY_N // unroll, step, 0)

    def stage_d(i, carry):
        for g in range(unroll):
            q2 = i * unroll + g
            y = jnp.dot(md_ref[q2], _cstack(get(pl.ds(q2, HY_N, stride=HY_PITCH))), preferred_element_type=F32)
            o_ref[pl.ds(pl.multiple_of(q2 * n1, 16), n1), :] = y.astype(o_ref.dtype)
        return carry

    stage_a(h_ref)
    mid(False)
    stage_a(z_ref)
    mid(True)
    lax.fori_loop(0, HY_N // unroll, stage_d, 0)


def hy_conv(z_t, h_t, tabs):
    ncb, L, lanes = z_t.shape
    blk = pl.BlockSpec((None, L, lanes), lambda cb: (cb, 0, 0), pipeline_mode=pl.Buffered(1))
    consts = [tabs["ma"], tabs["fb"], tabs["fc"], tabs["md"]]
    return pl.pallas_call(
        _hy_conv_kernel,
        grid=(ncb,),
        in_specs=[blk, blk] + [_const_spec(c.shape, True) for c in consts],
        out_specs=pl.BlockSpec((None, L, lanes), lambda cb: (cb, 0, 0)),
        out_shape=jax.ShapeDtypeStruct((ncb, L, lanes), BF16),
        scratch_shapes=[pltpu.VMEM((2, HY_N * HY_PITCH, HY_CB), F32), pltpu.VMEM((HY_N * HY_N, lanes), BF16)],
        compiler_params=_cparams("parallel"),
        name="hy_conv",
    )(z_t, h_t, *consts)


def _hy_gate_kernel(unpack, x_ref, y_ref, z_ref, b_ref, o_ref):
    z = x_ref[...].astype(F32) * (y_ref[...].astype(F32) + b_ref[...] * z_ref[...].astype(F32))
    if unpack:
        o_ref[0] = z[:, :HY_CB].astype(o_ref.dtype)
        o_ref[1] = z[:, HY_CB:].astype(o_ref.dtype)
    else:
        o_ref[...] = z.astype(o_ref.dtype)


def hy_gate(x, gx, y, z, gz, bias_p, unpack, rows=2048):
    ncb, L, lanes = y.shape
    blk = pl.BlockSpec((None, rows, lanes), lambda cb, r: (cb, r, 0))
    pick = lambda g: pl.BlockSpec((None, None, rows, lanes), lambda cb, r: (g, cb, r, 0))
    if unpack:
        out_spec = pl.BlockSpec((2, rows, HY_CB), lambda cb, r: (0, r, cb))
        out_shape = jax.ShapeDtypeStruct((2, L, ncb * HY_CB), BF16)
    else:
        out_spec, out_shape = blk, jax.ShapeDtypeStruct(y.shape, BF16)
    return pl.pallas_call(
        functools.partial(_hy_gate_kernel, unpack),
        grid=(ncb, L // rows),
        in_specs=[pick(gx), blk, pick(gz), pl.BlockSpec((None, 1, lanes), lambda cb, r: (cb, 0, 0))],
        out_specs=out_spec,
        out_shape=out_shape,
        compiler_params=_cparams("parallel", "parallel"),
        name="hy_gate",
    )(x, y, z, bias_p)


def _hy_ctx_kernel(s_ref, h_ref, b_ref, ff_ref, fi_ref, o_ref):
    z = s_ref[0].astype(F32)
    L = z.shape[0]
    ff = ff_ref[...]
    for o in range(2):
        hp = jnp.concatenate([h_ref[o], jnp.zeros((L, HY_CB), F32)], axis=1)
        spec = jnp.dot(ff, _cstack(hp.astype(BF16)), preferred_element_type=F32)
        w = jnp.dot(ff, _cstack(z.astype(BF16)), preferred_element_type=F32)
        y = jnp.dot(fi_ref[...], _cstack(_cmul(spec, w).astype(BF16)), preferred_element_type=F32)
        z = s_ref[o + 1].astype(F32) * (y + b_ref[o] * z)
    o_ref[0] = z[:, :HY_CB].astype(o_ref.dtype)
    o_ref[1] = z[:, HY_CB:].astype(o_ref.dtype)


def hy_ctx(s, h, bias_p, ff, fi):
    _, ncb, L, _ = s.shape
    return pl.pallas_call(
        _hy_ctx_kernel,
        grid=(ncb,),
        in_specs=[
            pl.BlockSpec((3, None, L, 2 * HY_CB), lambda cb: (0, cb, 0, 0)),
            pl.BlockSpec((2, L, HY_CB), lambda cb: (0, 0, cb)),
            pl.BlockSpec((2, None, 1, 2 * HY_CB), lambda cb: (0, cb, 0, 0)),
            _const_spec(ff.shape), _const_spec(fi.shape),
        ],
        out_specs=pl.BlockSpec((2, L, HY_CB), lambda cb: (0, 0, cb)),
        out_shape=jax.ShapeDtypeStruct((2, L, ncb * HY_CB), BF16),
        compiler_params=_cparams("parallel"),
        name="hy_ctx",
    )(s, h, bias_p, ff, fi)


def _hy_tables(L, L_ctx):
    n = 2 * L
    n1 = L // HY_N
    two_pi = 2.0 * math.pi
    ar = lambda m: jnp.arange(m, dtype=jnp.int32)
    cs = lambda m, period: (jnp.cos(m.astype(F32) * (two_pi / period)), jnp.sin(m.astype(F32) * (two_pi / period)))
    tc, ts = cs((ar(HY_N)[:, None] * ar(HY_N)[None, :]) % n, n)
    c1, s1 = cs((ar(HY_N)[:, None] * ar(n1)[None, :]) % HY_N, HY_N)
    c = c1[None] * tc[:, :, None] - s1[None] * ts[:, :, None]
    s = s1[None] * tc[:, :, None] + c1[None] * ts[:, :, None]
    ma = jnp.concatenate([c, s], axis=-1).astype(BF16)
    q1 = ar(n1) + n1 // 2
    c1, s1 = cs((q1[:, None] * ar(HY_N)[None, :]) % HY_N, HY_N)
    c = c1[None] * tc[:, None, :] - s1[None] * ts[:, None, :]
    s = s1[None] * tc[:, None, :] + c1[None] * ts[:, None, :]
    md = (jnp.concatenate([c, -s], axis=-1) / n).astype(BF16)
    m = (ar(HY_N)[:, None] * ar(HY_N)[None, :]) % HY_N
    c, s = cs(m, HY_N)
    fb = jnp.concatenate([c, s], axis=-1).astype(BF16)
    fc = jnp.concatenate([c, -s], axis=-1).astype(BF16)
    nc = 2 * L_ctx
    m = (ar(nc)[:, None] * ar(L_ctx)[None, :]) % nc
    c, s = cs(m, nc)
    ff = jnp.concatenate([c, s], axis=-1).astype(BF16)
    m = ((ar(L_ctx) + L_ctx // 2)[:, None] * ar(nc)[None, :]) % nc
    c, s = cs(m, nc)
    fi = (jnp.concatenate([c, -s], axis=-1) / nc).astype(BF16)
    return {"ma": ma, "md": md, "fb": fb, "fc": fc, "ff": ff, "fi": fi}


def hyena(u_hy, short_w, short_b, filt, bias, tabs, n_lat, n_ctx):
    bias_p = jnp.concatenate([bias.reshape(2, HY_NCB, 1, HY_CB)] * 2, axis=-1)
    n1 = n_lat // HY_N
    to_slabs = lambda t: jnp.swapaxes(t.reshape(HY_NCB, n1, HY_N, -1), 1, 2).reshape(HY_NCB, n_lat, -1)
    to_time = lambda t: jnp.swapaxes(t.reshape(HY_NCB, HY_N, n1, -1), 1, 2).reshape(HY_NCB, n_lat, -1)
    h_lat = _hyena_filters(n_lat, *filt)
    h_lat = jnp.transpose(h_lat.reshape(2, n1, HY_N, HY_NCB, HY_CB), (0, 3, 2, 1, 4)).reshape(2, HY_NCB, n_lat, HY_CB)
    h_lat = jnp.concatenate([h_lat, jnp.zeros_like(h_lat)], axis=-1).astype(BF16)
    s = hy_short(u_hy, short_w, short_b, 0, n_lat)
    z, gz = s, 0
    for o in range(2):
        y = hy_conv(to_slabs(z[gz]), h_lat[o], tabs)
        z, gz = hy_gate(s, o + 1, to_time(y), z, gz, bias_p[o], o == 1)[None], 0
    s_c = hy_short(u_hy, short_w, short_b, n_lat // n_ctx, n_ctx)
    z_c = hy_ctx(s_c, _hyena_filters(n_ctx, *filt), bias_p, tabs["ff"], tabs["fi"])
    return jnp.concatenate([z[0], z_c], axis=1)


def _rope_tables(n_lat, n_ctx):
    rows = n_lat // GRID_W
    inv_freq = ROPE_BASE ** (-jnp.arange(0, ROPE_AXIS, 2, dtype=F32) / ROPE_AXIS)
    ang_r = jnp.arange(rows, dtype=F32)[:, None] * inv_freq[None, :]
    ang_c = jnp.arange(GRID_W, dtype=F32)[:, None] * inv_freq[None, :]
    nf = ROPE_AXIS // 2
    per_row = lambda t: jnp.broadcast_to(t[:, None, :], (rows, GRID_W, nf)).reshape(n_lat, nf)
    per_col = lambda t: jnp.broadcast_to(t[None, :, :], (rows, GRID_W, nf)).reshape(n_lat, nf)
    cos = jnp.concatenate([per_row(jnp.cos(ang_r)), per_col(jnp.cos(ang_c))], axis=-1)
    sin = jnp.concatenate([per_row(jnp.sin(ang_r)), per_col(jnp.sin(ang_c))], axis=-1)
    cos = jnp.concatenate([cos, jnp.ones((n_ctx, ROPE_AXIS), F32)], axis=0)
    sin = jnp.concatenate([sin, jnp.zeros((n_ctx, ROPE_AXIS), F32)], axis=0)
    T = n_lat + n_ctx
    z8 = jnp.zeros((T, ROPE_AXIS // 2), F32)
    cos32 = jnp.concatenate([cos[:, :8], cos[:, :8], cos[:, 8:], cos[:, 8:]], axis=-1)
    first = jnp.concatenate([-sin[:, :8], z8, -sin[:, 8:], z8], axis=-1)
    second = jnp.concatenate([z8, sin[:, :8], z8, sin[:, 8:]], axis=-1)
    pad = lambda t, fill: jnp.concatenate(
        [jnp.full((T, MLA_NOPE), fill, F32), t, jnp.zeros((T, HEAD_LANES - MLA_NOPE - MLA_ROPE), F32)], axis=-1)
    lane = jnp.arange(HEAD_LANES)
    seg = jnp.where(lane < MLA_NOPE, 0, jnp.where(lane < MLA_NOPE + MLA_ROPE, 1, 2))
    msq = (seg[:, None] == seg[None, :]) & (seg[:, None] < 2)
    msq = jnp.where(msq, jnp.where(seg[:, None] == 0, 1.0 / MLA_NOPE, 1.0 / MLA_ROPE), 0.0).astype(BF16)
    return {"ct": pad(cos32, 1.0), "s1": pad(first, 0.0), "s2": pad(second, 0.0), "msq": msq}


def _head_gain(g, scale):
    return jnp.concatenate([g * scale, jnp.zeros((HEAD_LANES - g.shape[0],), F32)])[None, :]


def _even_weights(j, ev_w_in, mla_q_norm_g, mla_w_uq, mla_kv_norm_g, mla_w_ukv, mla_qk_g_q, mla_qk_g_k):
    i_q = 3 * D_HY
    i_kv = i_q + MLA_Q_RANK
    i_kr = i_kv + MLA_KV_RANK
    w_in = ev_w_in[j]
    d = w_in.shape[0]
    kr_group = jnp.concatenate([jnp.zeros((d, MLA_NOPE), F32), w_in[:, i_kr:],
                                jnp.zeros((d, HEAD_LANES - MLA_NOPE - MLA_ROPE), F32)], axis=-1)
    w_in_p = jnp.concatenate([w_in[:, :i_kr], kr_group], axis=-1).astype(BF16)
    hq = MLA_NOPE + MLA_ROPE
    w_uq = mla_w_uq[j].reshape(MLA_Q_RANK, MLA_HEADS, hq)
    w_uq = jnp.pad(w_uq, ((0, 0), (0, 0), (0, HEAD_LANES - hq))).reshape(MLA_Q_RANK, -1).astype(BF16)
    w_ukv = mla_w_ukv[j].reshape(MLA_KV_RANK, MLA_HEADS, MLA_NOPE + MLA_V)
    w_uk = jnp.pad(w_ukv[:, :, :MLA_NOPE], ((0, 0), (0, 0), (0, HEAD_LANES - MLA_NOPE)))
    w_uk = w_uk.reshape(MLA_KV_RANK, -1).astype(BF16)
    w_uv = w_ukv[:, :, MLA_NOPE:].reshape(MLA_KV_RANK, -1).astype(BF16)
    return {
        "w_in": w_in_p, "q_norm_g": mla_q_norm_g[j][None, :], "kv_norm_g": mla_kv_norm_g[j][None, :],
        "w_uq": w_uq, "w_uk": w_uk, "w_uv": w_uv,
        "gq": _head_gain(mla_qk_g_q[j], MLA_SCALE * math.log2(math.e)), "gk": _head_gain(mla_qk_g_k[j], 1.0),
    }


def _rg_gate_weights(j, rg_w_a, rg_b_a, rg_w_x, rg_b_x, rg_lambda):
    wg = 0.5 * jnp.concatenate([rg_w_a[j], rg_w_x[j]], axis=-1)
    wg = jnp.transpose(wg, (1, 0, 2, 3)).astype(BF16)
    blk = lambda t: jnp.transpose(t.reshape(2, RG_BLOCKS, RG_BW), (1, 0, 2))
    bg = 0.5 * jnp.concatenate([blk(rg_b_a[j]), blk(rg_b_x[j])], axis=-1)[:, :, None, :]
    lam = blk(rg_lambda[j])[:, :, None, :]
    return wg, bg, lam


def kernel(x, c, ctx, c_ctx, ada_w, ada_b, norm1_g, norm2_g, ffn_w_in, ffn_w_out, ev_w_in, hy_short_w, hy_short_b, hy_w1, hy_b1, hy_w2, hy_b2, hy_w3, hy_freq, hy_bias, mla_q_norm_g, mla_w_uq, mla_kv_norm_g, mla_w_ukv, mla_qk_g_q, mla_qk_g_k, ev_w_out, rg_w_in, rg_conv_w, rg_conv_b, rg_w_a, rg_b_a, rg_w_x, rg_b_x, rg_lambda, rg_w_out):
    B, n_lat, D = x.shape
    n_ctx = ctx.shape[1]
    T = n_lat + n_ctx
    bm = 768
    assert T % bm == 0 and n_lat % 512 == 0 and n_ctx == 256

    cond8 = jnp.concatenate([c, c_ctx[None, :], jnp.zeros((8 - B - 1, D), F32)], axis=0)
    mods = ada_modulation(cond8, ada_w, ada_b)
    tabs = _rope_tables(n_lat, n_ctx)
    hy_tabs = _hy_tables(n_lat, n_ctx)
    x_all = jnp.concatenate([x, ctx], axis=1)

    for i in range(DEPTH):
        j = i // 2
        last = i == DEPTH - 1
        m = mods[i]
        g1 = norm1_g[i][None, :]
        g2 = norm2_g[i][None, :]
        if i % 2 == 0:
            w = _even_weights(j, ev_w_in, mla_q_norm_g, mla_w_uq, mla_kv_norm_g, mla_w_ukv, mla_qk_g_q, mla_qk_g_k)
            u_hy, q, k, v = even_pre(x_all, m, g1, w, tabs, n_lat, bm)
            att = mla_attend(q, k, v, n_lat)
            filt = (hy_w1[j], hy_b1[j], hy_w2[j], hy_b2[j], hy_w3[j], hy_freq[j])
            hy = hyena(u_hy, hy_short_w[j], hy_short_b[j][None, :], filt, hy_bias[j], hy_tabs, n_lat, n_ctx)
            w_out = ev_w_out[j].astype(BF16)
            mixes = [(hy, w_out[:D_HY]), (att, w_out[D_HY:])]
        else:
            gate, xr = odd_pre(x_all, m, g1, rg_w_in[j].astype(BF16), n_lat, bm)
            wg, bg, lam = _rg_gate_weights(j, rg_w_a, rg_b_a, rg_w_x, rg_b_x, rg_lambda)
            y = rglru(gate, xr, rg_conv_w[j], rg_conv_b[j][None, :], wg, bg, lam, n_lat)
            mixes = [(y, rg_w_out[j].astype(BF16))]
        wg_f = ffn_w_in[i][:, :D_FF].astype(BF16)
        wu_f = ffn_w_in[i][:, D_FF:].astype(BF16)
        wd_f = ffn_w_out[i].astype(BF16)
        if last:
            x_all = post_mixer(x_all, m, g2, mixes, wg_f, wu_f, wd_f, n_lat, n_lat, 512)
        else:
            x_all = post_mixer(x_all, m, g2, mixes, wg_f, wu_f, wd_f, n_lat, T, bm)
    return x_all
```
